```python
import jax, jax.numpy as jnp
from jax import lax
import numpy as np

D_MODEL = 2048
BATCH = 4
SEQ = 2048
DEPTH = 4

CHUNK = 64
N_MEM = 256
D_MIX = D_MODEL
DN_HEAD_DIM = 128
DN_HEADS = (D_MIX // 2) // DN_HEAD_DIM
DN_WIDTH = DN_HEADS * DN_HEAD_DIM
DN_CONV = 4
MLA_NOPE = 128
MLA_ROPE = 64
MLA_V = 128
MLA_HEADS = (D_MIX - DN_WIDTH) // MLA_V
MLA_Q_RANK = 512
MLA_KV_RANK = 256
ROPE_BASE = 10000.0
Q_BLOCK = 128
XA_HEADS = 4
XA_HEAD_DIM = D_MODEL // XA_HEADS
D_FF = ((8 * D_MODEL // 3 + 255) // 256) * 256
FFN_CONV = 3
EPS = 1e-6

kernel_name = "hybrid_deltanet_mla_memxattn_convffn"


def _in_split_sizes():
    return [DN_WIDTH, DN_WIDTH, DN_WIDTH, DN_WIDTH,
            DN_HEADS, DN_HEADS,
            MLA_Q_RANK,
            MLA_KV_RANK + MLA_ROPE]


def _in_cols():
    return sum(_in_split_sizes())


def _split_points():
    return [int(v) for v in np.cumsum(_in_split_sizes())[:-1]]


def rms_norm(x, gain):
    xf = x.astype(jnp.float32)
    y = xf * lax.rsqrt(jnp.mean(xf * xf, axis=-1, keepdims=True) + EPS)
    return (y * gain.astype(jnp.float32)).astype(x.dtype)


def l2_norm(x):
    xf = x.astype(jnp.float32)
    return xf * lax.rsqrt(jnp.sum(xf * xf, axis=-1, keepdims=True) + EPS)


def causal_dwconv(x, w):
    K, C = w.shape
    return lax.conv_general_dilated(
        x, w[:, None, :].astype(x.dtype), window_strides=(1,),
        padding=[(K - 1, 0)], dimension_numbers=("NWC", "WIO", "NWC"),
        feature_group_count=C)


def rope_cos_sin(positions):
    inv = ROPE_BASE ** (-jnp.arange(0, MLA_ROPE, 2, dtype=jnp.float32) / MLA_ROPE)
    ang = positions.astype(jnp.float32)[..., None] * inv
    return jnp.cos(ang), jnp.sin(ang)


def apply_rope(x, cos, sin):
    xf = x.astype(jnp.float32)
    x1, x2 = jnp.split(xf, 2, axis=-1)
    return jnp.concatenate([x1 * cos - x2 * sin, x2 * cos + x1 * sin], axis=-1).astype(x.dtype)


def chunk_gated_delta_rule(q, k, v, g, beta):
    B, S, H, Dk = q.shape
    Dv = v.shape[-1]
    N = S // CHUNK
    f32 = jnp.float32

    def chunks(t):
        return jnp.moveaxis(t.reshape(B, N, CHUNK, H, *t.shape[3:]), 3, 2)

    q = chunks(q.astype(f32)) * (Dk ** -0.5)
    k = chunks(k.astype(f32))
    v = chunks(v.astype(f32))
    beta = chunks(beta.astype(f32))
    G = jnp.cumsum(chunks(g.astype(f32)), axis=-1)

    incl = np.tril(np.ones((CHUNK, CHUNK), dtype=bool))
    strict = np.tril(np.ones((CHUNK, CHUNK), dtype=bool), -1)
    decay = jnp.exp(jnp.where(incl, G[..., :, None] - G[..., None, :], -jnp.inf))

    kb = k * beta[..., None]
    lower = jnp.where(strict, jnp.einsum("bnhik,bnhjk->bnhij", kb, k) * decay, 0.0)
    a_mat = lower + np.eye(CHUNK, dtype=np.float32)
    rhs = jnp.concatenate([v * beta[..., None], kb * jnp.exp(G)[..., None]], axis=-1)
    sol = lax.linalg.triangular_solve(a_mat, rhs, left_side=True, lower=True,
                                      unit_diagonal=True)
    u, w = sol[..., :Dv], sol[..., Dv:]

    attn = jnp.einsum("bnhik,bnhjk->bnhij", q, k) * decay
    q_dec = q * jnp.exp(G)[..., None]
    k_dec = k * jnp.exp(G[..., -1:] - G)[..., None]
    g_last = jnp.exp(G[..., -1])

    def step(state, xs):
        u_c, w_c, a_c, q_c, k_c, gl_c = xs
        v_new = u_c - jnp.einsum("bhck,bhkv->bhcv", w_c, state)
        o_c = (jnp.einsum("bhck,bhkv->bhcv", q_c, state)
               + jnp.einsum("bhij,bhjv->bhiv", a_c, v_new))
        state = state * gl_c[..., None, None] + jnp.einsum("bhck,bhcv->bhkv", k_c, v_new)
        return state, o_c

    xs = tuple(jnp.moveaxis(t, 1, 0) for t in (u, w, attn, q_dec, k_dec, g_last))
    state0 = jnp.zeros((B, H, Dk, Dv), f32)
    _, o = lax.scan(step, state0, xs)
    o = jnp.moveaxis(o, 0, 1)
    return jnp.moveaxis(o, 2, 3).reshape(B, S, H, Dv)


def gated_deltanet(q_raw, k_raw, v_raw, z, b, a, conv_w, a_log, dt_bias, out_norm):
    B, S, _ = q_raw.shape
    qkv = jax.nn.silu(causal_dwconv(jnp.concatenate([q_raw, k_raw, v_raw], axis=-1), conv_w))
    q, k, v = jnp.split(qkv, 3, axis=-1)
    q = l2_norm(q.reshape(B, S, DN_HEADS, DN_HEAD_DIM))
    k = l2_norm(k.reshape(B, S, DN_HEADS, DN_HEAD_DIM))
    v = v.reshape(B, S, DN_HEADS, DN_HEAD_DIM)
    beta = jax.nn.sigmoid(b.astype(jnp.float32))
    g = -jnp.exp(a_log.astype(jnp.float32)) * jax.nn.softplus(
        a.astype(jnp.float32) + dt_bias.astype(jnp.float32))
    o = chunk_gated_delta_rule(q, k, v, g, beta)
    zf = z.reshape(B, S, DN_HEADS, DN_HEAD_DIM).astype(jnp.float32)
    o = rms_norm(o, out_norm) * jax.nn.silu(zf)
    return o.reshape(B, S, DN_WIDTH).astype(q_raw.dtype)


def mla_attention(q_lat, kv_lat, q_norm, w_qb, kv_norm, w_kvb, cos, sin):
    B, S, _ = q_lat.shape
    q = (rms_norm(q_lat, q_norm) @ w_qb).reshape(B, S, MLA_HEADS, MLA_NOPE + MLA_ROPE)
    q_nope = q[..., :MLA_NOPE]
    q_pe = apply_rope(q[..., MLA_NOPE:], cos[:, :, None, :], sin[:, :, None, :])
    c_kv = kv_lat[..., :MLA_KV_RANK]
    k_pe = apply_rope(kv_lat[..., MLA_KV_RANK:], cos, sin)
    kv = (rms_norm(c_kv, kv_norm) @ w_kvb).reshape(B, S, MLA_HEADS, MLA_NOPE + MLA_V)
    k_nope, v = kv[..., :MLA_NOPE], kv[..., MLA_NOPE:]
    scale = (MLA_NOPE + MLA_ROPE) ** -0.5
    outs = []
    for blk in range(S // Q_BLOCK):
        q0 = blk * Q_BLOCK
        kend = q0 + Q_BLOCK
        s = (jnp.einsum("bqhd,bkhd->bhqk", q_nope[:, q0:kend], k_nope[:, :kend])
             + jnp.einsum("bqhr,bkr->bhqk", q_pe[:, q0:kend], k_pe[:, :kend]))
        s = s.astype(jnp.float32) * scale
        q_chunk = (q0 + np.arange(Q_BLOCK)) // CHUNK
        k_chunk = np.arange(kend) // CHUNK
        mask = k_chunk[None, :] <= q_chunk[:, None]
        p = jax.nn.softmax(jnp.where(mask, s, -jnp.inf), axis=-1).astype(v.dtype)
        outs.append(jnp.einsum("bhqk,bkhd->bqhd", p, v[:, :kend]))
    o = jnp.concatenate(outs, axis=1)
    return o.reshape(B, S, MLA_HEADS * MLA_V)


def memory_cross_attention(h, mem_n, wq, wk, wv, wo):
    B, S, _ = h.shape
    M = mem_n.shape[1]
    q = (h @ wq).reshape(B, S, XA_HEADS, XA_HEAD_DIM)
    k = (mem_n @ wk).reshape(B, M, XA_HEADS, XA_HEAD_DIM)
    v = (mem_n @ wv).reshape(B, M, XA_HEADS, XA_HEAD_DIM)
    s = jnp.einsum("bqhd,bmhd->bhqm", q, k).astype(jnp.float32) * (XA_HEAD_DIM ** -0.5)
    p = jax.nn.softmax(s, axis=-1).astype(v.dtype)
    o = jnp.einsum("bhqm,bmhd->bqhd", p, v).reshape(B, S, XA_HEADS * XA_HEAD_DIM)
    return o @ wo


def conv_ffn(h, w_up, conv_w, conv_b, w_down):
    u = causal_dwconv(h @ w_up, conv_w) + conv_b
    gate, up = jnp.split(u, 2, axis=-1)
    return (jax.nn.silu(gate) * up) @ w_down


def setup_inputs(seed: int = 0) -> dict:
    key = jax.random.key(seed)
    ks = iter(jax.random.split(key, 40))
    f32 = jnp.float32

    def dense(shape, fan_in, scale=1.0):
        return jax.random.normal(next(ks), shape, f32) * (scale * fan_in ** -0.5)

    def gain(shape):
        return 1.0 + 0.01 * jax.random.normal(next(ks), shape, f32)

    x = jax.random.normal(next(ks), (BATCH, SEQ, D_MODEL), f32)
    mem = jax.random.normal(next(ks), (BATCH, N_MEM, D_MODEL), f32)
    offsets = jax.random.randint(next(ks), (BATCH, 1), 0, 64) * CHUNK
    positions = (offsets + jnp.arange(SEQ, dtype=jnp.int32)[None, :]).astype(jnp.int32)

    a_log = jnp.log(jax.random.uniform(next(ks), (DEPTH, DN_HEADS), f32, 1.0, 16.0))
    dt = jnp.exp(jax.random.uniform(next(ks), (DEPTH, DN_HEADS), f32,
                                    float(np.log(1e-3)), float(np.log(1e-1))))
    dt_bias = dt + jnp.log(-jnp.expm1(-dt))
    out_scale = 0.5
    return {
        "x": x,
        "mem": mem,
        "positions": positions,
        "norm_mix": gain((DEPTH, D_MODEL)),
        "w_in": dense((DEPTH, D_MODEL, _in_cols()), D_MODEL),
        "dn_conv": dense((DEPTH, DN_CONV, 3 * DN_WIDTH), DN_CONV),
        "dn_a_log": a_log,
        "dn_dt_bias": dt_bias,
        "dn_out_norm": gain((DEPTH, DN_HEAD_DIM)),
        "mla_q_norm": gain((DEPTH, MLA_Q_RANK)),
        "mla_w_qb": dense((DEPTH, MLA_Q_RANK, MLA_HEADS * (MLA_NOPE + MLA_ROPE)), MLA_Q_RANK),
        "mla_kv_norm": gain((DEPTH, MLA_KV_RANK)),
        "mla_w_kvb": dense((DEPTH, MLA_KV_RANK, MLA_HEADS * (MLA_NOPE + MLA_V)), MLA_KV_RANK),
        "w_out": dense((DEPTH, D_MIX, D_MODEL), D_MIX, out_scale),
        "mem_norm": gain((D_MODEL,)),
        "norm_xattn": gain((DEPTH, D_MODEL)),
        "xa_wq": dense((DEPTH, D_MODEL, XA_HEADS * XA_HEAD_DIM), D_MODEL),
        "xa_wk": dense((DEPTH, D_MODEL, XA_HEADS * XA_HEAD_DIM), D_MODEL),
        "xa_wv": dense((DEPTH, D_MODEL, XA_HEADS * XA_HEAD_DIM), D_MODEL),
        "xa_wo": dense((DEPTH, XA_HEADS * XA_HEAD_DIM, D_MODEL), D_MODEL, out_scale),
        "norm_ffn": gain((DEPTH, D_MODEL)),
        "ffn_w_up": dense((DEPTH, D_MODEL, 2 * D_FF), D_MODEL),
        "ffn_conv": dense((DEPTH, FFN_CONV, 2 * D_FF), FFN_CONV),
        "ffn_conv_bias": 0.01 * jax.random.normal(next(ks), (DEPTH, 2 * D_FF), f32),
        "ffn_w_down": dense((DEPTH, D_FF, D_MODEL), D_FF, out_scale),
        "norm_final": gain((D_MODEL,)),
    }


def reference(x, mem, positions, norm_mix, w_in, dn_conv, dn_a_log, dn_dt_bias,
              dn_out_norm, mla_q_norm, mla_w_qb, mla_kv_norm, mla_w_kvb, w_out,
              mem_norm, norm_xattn, xa_wq, xa_wk, xa_wv, xa_wo, norm_ffn,
              ffn_w_up, ffn_conv, ffn_conv_bias, ffn_w_down, norm_final):
    cos, sin = rope_cos_sin(positions)
    mem_n = rms_norm(mem, mem_norm)
    split_points = _split_points()
    h = x
    for l in range(DEPTH):
        u = rms_norm(h, norm_mix[l])
        proj = u @ w_in[l]
        dq, dk, dv, dz, db, da, mq, mkv = jnp.split(proj, split_points, axis=-1)
        o_dn = gated_deltanet(dq, dk, dv, dz, db, da, dn_conv[l], dn_a_log[l],
                              dn_dt_bias[l], dn_out_norm[l])
        o_mla = mla_attention(mq, mkv, mla_q_norm[l], mla_w_qb[l], mla_kv_norm[l],
                              mla_w_kvb[l], cos, sin)
        h = h + jnp.concatenate([o_dn.astype(h.dtype), o_mla.astype(h.dtype)], axis=-1) @ w_out[l]
        h = h + memory_cross_attention(rms_norm(h, norm_xattn[l]), mem_n, xa_wq[l],
                                       xa_wk[l], xa_wv[l], xa_wo[l])
        h = h + conv_ffn(rms_norm(h, norm_ffn[l]), ffn_w_up[l], ffn_conv[l],
                         ffn_conv_bias[l], ffn_w_down[l])
    return rms_norm(h, norm_final)
```

```python
import functools

import numpy as np
import jax
import jax.numpy as jnp
from jax import lax
from jax.experimental import pallas as pl
from jax.experimental.pallas import tpu as pltpu

F32 = jnp.float32
BF16 = jnp.bfloat16

D_MODEL = 2048
DEPTH = 4
CHUNK = 64
DN_HEADS = 8
DN_HEAD_DIM = 128
DN_WIDTH = DN_HEADS * DN_HEAD_DIM
DN_CONV = 4
MLA_HEADS = 8
MLA_NOPE = 128
MLA_ROPE = 64
MLA_V = 128
MLA_Q_RANK = 512
MLA_KV_RANK = 256
ROPE_BASE = 10000.0
XA_HEADS = 4
XA_HEAD_DIM = D_MODEL // XA_HEADS
D_FF = 5632
FFN_CONV = 3
EPS = 1e-6

LANES = 128
SUBLANES = 8
VMEM_LIMIT = 56 * 1024 * 1024

SM_Q = 0
SM_CKV = MLA_Q_RANK
SM_KPE = SM_CKV + MLA_KV_RANK
SM_BA = SM_KPE + LANES
SM_WIDTH = SM_BA + LANES
MLA_QK = MLA_NOPE + LANES


def _params(*sem):
    return pltpu.CompilerParams(dimension_semantics=sem, vmem_limit_bytes=VMEM_LIMIT)


def _dot(a, b):
    return jnp.dot(a, b, preferred_element_type=F32)


def _dot_nt(a, b):
    return lax.dot_general(a, b, (((1,), (1,)), ((), ())), preferred_element_type=F32)


def _dot_tn(a, b):
    return lax.dot_general(a, b, (((0,), (0,)), ((), ())), preferred_element_type=F32)


def _dot_f32(a, b):
    return jnp.dot(a, b, preferred_element_type=F32, precision=lax.Precision.HIGHEST)


def _rms(x, gain):
    return x * lax.rsqrt(jnp.mean(x * x, axis=-1, keepdims=True) + EPS) * gain


def _silu(x):
    return x * jax.nn.sigmoid(x)


def _rmsnorm_kernel(x_ref, g_ref, o_ref):
    o_ref[...] = _rms(x_ref[...].astype(F32), g_ref[...]).astype(o_ref.dtype)


def rmsnorm_rows(x, gain, out_dtype, tm=512):
    m, d = x.shape
    return pl.pallas_call(
        _rmsnorm_kernel,
        grid=(m // tm,),
        in_specs=[pl.BlockSpec((tm, d), lambda i: (i, 0)),
                  pl.BlockSpec((1, d), lambda i: (0, 0))],
        out_specs=pl.BlockSpec((tm, d), lambda i: (i, 0)),
        out_shape=jax.ShapeDtypeStruct((m, d), out_dtype),
        compiler_params=_params("parallel"),
        name="rmsnorm_rows",
    )(x, gain.reshape(1, d))


def _mm_kernel(a_ref, w_ref, o_ref):
    o_ref[...] = _dot(a_ref[...], w_ref[...]).astype(o_ref.dtype)


def matmul(a, w, out_dtype, tm, tn, name="matmul"):
    m, k = a.shape
    n = w.shape[1]
    return pl.pallas_call(
        _mm_kernel,
        grid=(n // tn, m // tm),
        in_specs=[pl.BlockSpec((tm, k), lambda j, i: (i, 0)),
                  pl.BlockSpec((k, tn), lambda j, i: (0, j))],
        out_specs=pl.BlockSpec((tm, tn), lambda j, i: (i, j)),
        out_shape=jax.ShapeDtypeStruct((m, n), out_dtype),
        compiler_params=_params("parallel", "parallel"),
        name=name,
    )(a, w)


def _mm_res_norm_kernel(*refs, n_parts, nk, emit_h):
    a_refs = refs[:n_parts]
    w_ref, h_ref, g_ref = refs[n_parts:n_parts + 3]
    outs = refs[n_parts + 3:]
    if emit_h:
        ho_ref, uo_ref = outs[0], outs[1]
        rest = outs[2:]
    else:
        ho_ref, uo_ref = None, outs[0]
        rest = outs[1:]

    def finish(acc):
        h = h_ref[...] + acc
        if emit_h:
            ho_ref[...] = h
        uo_ref[...] = _rms(h, g_ref[...]).astype(uo_ref.dtype)

    if nk == 1:
        off = 0
        acc = None
        for a_ref in a_refs:
            kp = a_ref.shape[1]
            part = _dot(a_ref[...], w_ref[off:off + kp, :])
            acc = part if acc is None else acc + part
            off += kp
        finish(acc)
    else:
        acc_ref = rest[0]
        kk = pl.program_id(1)

        @pl.when(kk == 0)
        def _():
            acc_ref[...] = jnp.zeros_like(acc_ref)

        acc_ref[...] += _dot(a_refs[0][...], w_ref[...])

        @pl.when(kk == nk - 1)
        def _():
            finish(acc_ref[...])


def matmul_residual_norm(a_parts, w, h, gain, u_dtype, tm, tk=None, emit_h=True, name="mm_res_norm"):
    m, n = h.shape
    k = w.shape[0]
    n_parts = len(a_parts)
    if tk is None:
        tk = k
    nk = k // tk
    assert nk == 1 or n_parts == 1
    if nk == 1:
        a_specs = [pl.BlockSpec((tm, a.shape[1]), lambda i, kk: (i, 0)) for a in a_parts]
    else:
        a_specs = [pl.BlockSpec((tm, tk), lambda i, kk: (i, kk))]
    in_specs = a_specs + [
        pl.BlockSpec((tk, n), lambda i, kk: (kk, 0)),
        pl.BlockSpec((tm, n), lambda i, kk: (i, 0)),
        pl.BlockSpec((1, n), lambda i, kk: (0, 0)),
    ]
    out_specs = [pl.BlockSpec((tm, n), lambda i, kk: (i, 0))]
    out_shape = [jax.ShapeDtypeStruct((m, n), u_dtype)]
    if emit_h:
        out_specs = [pl.BlockSpec((tm, n), lambda i, kk: (i, 0))] + out_specs
        out_shape = [jax.ShapeDtypeStruct((m, n), F32)] + out_shape
    scratch = [] if nk == 1 else [pltpu.VMEM((tm, n), F32)]
    res = pl.pallas_call(
        functools.partial(_mm_res_norm_kernel, n_parts=n_parts, nk=nk, emit_h=emit_h),
        grid=(m // tm, nk),
        in_specs=in_specs,
        out_specs=out_specs,
        out_shape=out_shape,
        scratch_shapes=scratch,
        compiler_params=_params("parallel", "arbitrary"),
        name=name,
    )(*a_parts, w, h, gain.reshape(1, n))
    return tuple(res) if emit_h else res[0]


def _rope_table_kernel(pos_ref, inv_ref, c_ref, sa_ref, sb_ref):
    ang = pos_ref[...].astype(F32) * inv_ref[...]
    c = jnp.cos(ang)
    s = jnp.sin(ang)
    lane = lax.broadcasted_iota(jnp.int32, ang.shape, 1)
    half = MLA_ROPE // 2
    c_ref[...] = jnp.where(lane < MLA_ROPE, c, 0.0)
    sa_ref[...] = jnp.where(lane < half, -s, 0.0)
    sb_ref[...] = jnp.where((lane >= half) & (lane < MLA_ROPE), s, 0.0)


def rope_tables(positions, tm=512):
    t = positions.size
    inv = ROPE_BASE ** (-jnp.arange(0, MLA_ROPE, 2, dtype=F32) / MLA_ROPE)
    inv_row = jnp.concatenate([inv, inv, jnp.zeros((LANES - MLA_ROPE,), F32)]).reshape(1, LANES)
    spec = pl.BlockSpec((tm, LANES), lambda i: (i, 0))
    shape = jax.ShapeDtypeStruct((t, LANES), F32)
    return pl.pallas_call(
        _rope_table_kernel,
        grid=(t // tm,),
        in_specs=[pl.BlockSpec((tm, 1), lambda i: (i, 0)),
                  pl.BlockSpec((1, LANES), lambda i: (0, 0))],
        out_specs=[spec, spec, spec],
        out_shape=[shape, shape, shape],
        compiler_params=_params("parallel"),
        name="rope_tables",
    )(positions.reshape(t, 1), inv_row)


def _rope(x, c, sa, sb):
    half = MLA_ROPE // 2
    return x * c + pltpu.roll(x, LANES - half, 1) * sa + pltpu.roll(x, half, 1) * sb


def _deltanet_kernel(x_ref, ba_ref, cw_ref, alog_ref, dtb_ref, onorm_ref, o_ref, xs_ref, state_ref):
    c_idx = pl.program_id(1)
    C, D, H = CHUNK, DN_HEAD_DIM, DN_HEADS
    W3 = 3 * DN_WIDTH

    @pl.when(c_idx == 0)
    def _():
        xs_ref[0:SUBLANES, :] = jnp.zeros((SUBLANES, W3), F32)
        state_ref[...] = jnp.zeros_like(state_ref)

    x = x_ref[:, 0:W3]
    xs_ref[SUBLANES:SUBLANES + C, :] = x
    cw = cw_ref[...]
    y = cw[DN_CONV - 1:DN_CONV, :] * x
    for j in range(DN_CONV - 1):
        shift = DN_CONV - 1 - j
        y = y + cw[j:j + 1, :] * xs_ref[SUBLANES - shift:SUBLANES - shift + C, :]
    xs_ref[0:SUBLANES, :] = x[C - SUBLANES:C, :]
    y = _silu(y)

    ba = ba_ref[...]
    beta_all = jax.nn.sigmoid(ba)
    sp_in = ba + dtb_ref[...]
    softplus = jnp.maximum(sp_in, 0.0) + jnp.log1p(jnp.exp(-jnp.abs(sp_in)))
    g_all = -jnp.exp(alog_ref[...]) * softplus
    row = lax.broadcasted_iota(jnp.int32, (C, LANES), 0)
    G = g_all
    step = 1
    while step < C:
        G = G + jnp.where(row >= step, pltpu.roll(G, step, 0), 0.0)
        step *= 2
    GT = G.T
    g_last = G[C - 1:C, :]
    e_g = jnp.exp(G)
    e_rem = jnp.exp(g_last - G)
    e_last = jnp.exp(g_last)

    ri = lax.broadcasted_iota(jnp.int32, (C, C), 0)
    ci = lax.broadcasted_iota(jnp.int32, (C, C), 1)
    incl = ri >= ci
    strict = ri > ci
    eye = jnp.where(ri == ci, 1.0, 0.0).astype(F32)
    onorm = onorm_ref[...]

    for h in range(H):
        gl = H + h
        q = y[:, h * D:(h + 1) * D]
        k = y[:, DN_WIDTH + h * D:DN_WIDTH + (h + 1) * D]
        v = y[:, 2 * DN_WIDTH + h * D:2 * DN_WIDTH + (h + 1) * D]
        q = q * lax.rsqrt(jnp.sum(q * q, axis=-1, keepdims=True) + EPS) * (D ** -0.5)
        k = k * lax.rsqrt(jnp.sum(k * k, axis=-1, keepdims=True) + EPS)
        beta = beta_all[:, h:h + 1]
        g_col = G[:, gl:gl + 1]
        g_row = GT[gl:gl + 1, :]
        decay = jnp.exp(jnp.where(incl, g_col - g_row, -jnp.inf))
        kb = k * beta
        k16 = k.astype(BF16)
        lower = jnp.where(strict, _dot_nt(kb.astype(BF16), k16) * decay, 0.0)
        t_inv = eye - lower
        p = _dot_f32(lower, lower)
        sq = 2
        while True:
            t_inv = t_inv + _dot_f32(t_inv, p)
            sq *= 2
            if sq >= C:
                break
            p = _dot_f32(p, p)
        rhs = jnp.concatenate([v * beta, kb * e_g[:, gl:gl + 1]], axis=1)
        sol = _dot(t_inv.astype(BF16), rhs.astype(BF16))
        u = sol[:, 0:D]
        w = sol[:, D:2 * D]
        attn = _dot_nt(q.astype(BF16), k16) * decay
        s_old = state_ref[h]
        s16 = s_old.astype(BF16)
        v_new = u - _dot(w.astype(BF16), s16)
        vn16 = v_new.astype(BF16)
        o = _dot((q * e_g[:, gl:gl + 1]).astype(BF16), s16) + _dot(attn.astype(BF16), vn16)
        k_dec = (k * e_rem[:, gl:gl + 1]).astype(BF16)
        state_ref[h] = s_old * e_last[:, gl:gl + 1] + _dot_tn(k_dec, vn16)
        z = x_ref[:, W3 + h * D:W3 + (h + 1) * D]
        o_ref[:, h * D:(h + 1) * D] = (_rms(o, onorm) * _silu(z)).astype(o_ref.dtype)


def deltanet(proj_main, proj_small, conv_w, a_log, dt_bias, out_norm, batch, seq):
    t = batch * seq
    nc = seq // CHUNK
    pad = jnp.zeros((LANES - 2 * DN_HEADS,), F32)
    alog_row = jnp.concatenate([jnp.zeros((DN_HEADS,), F32), a_log, pad]).reshape(1, LANES)
    dtb_row = jnp.concatenate([jnp.zeros((DN_HEADS,), F32), dt_bias, pad]).reshape(1, LANES)
    w4 = 4 * DN_WIDTH
    return pl.pallas_call(
        _deltanet_kernel,
        grid=(batch, nc),
        in_specs=[pl.BlockSpec((CHUNK, w4), lambda b, c: (b * nc + c, 0)),
                  pl.BlockSpec((CHUNK, LANES), lambda b, c: (b * nc + c, SM_BA // LANES)),
                  pl.BlockSpec((DN_CONV, 3 * DN_WIDTH), lambda b, c: (0, 0)),
                  pl.BlockSpec((1, LANES), lambda b, c: (0, 0)),
                  pl.BlockSpec((1, LANES), lambda b, c: (0, 0)),
                  pl.BlockSpec((1, DN_HEAD_DIM), lambda b, c: (0, 0))],
        out_specs=pl.BlockSpec((CHUNK, DN_WIDTH), lambda b, c: (b * nc + c, 0)),
        out_shape=jax.ShapeDtypeStruct((t, DN_WIDTH), BF16),
        scratch_shapes=[pltpu.VMEM((SUBLANES + CHUNK, 3 * DN_WIDTH), F32),
                        pltpu.VMEM((DN_HEADS, DN_HEAD_DIM, DN_HEAD_DIM), F32)],
        compiler_params=_params("arbitrary", "arbitrary"),
        name="deltanet",
    )(proj_main, proj_small, conv_w, alog_row, dtb_row, out_norm.reshape(1, DN_HEAD_DIM))


def _mla_prep_kernel(x_ref, c_ref, sa_ref, sb_ref, qn_ref, kvn_ref, wq_ref, wkv_ref, q_ref, k_ref, v_ref):
    c, sa, sb = c_ref[...], sa_ref[...], sb_ref[...]
    q_lat = _rms(x_ref[:, SM_Q:SM_Q + MLA_Q_RANK], qn_ref[...])
    qf = _dot(q_lat.astype(BF16), wq_ref[...])
    c_kv = _rms(x_ref[:, SM_CKV:SM_CKV + MLA_KV_RANK], kvn_ref[...])
    kvf = _dot(c_kv.astype(BF16), wkv_ref[...])
    k_pe = _rope(x_ref[:, SM_KPE:SM_KPE + LANES], c, sa, sb).astype(k_ref.dtype)
    for h in range(MLA_HEADS):
        o = h * MLA_QK
        q_ref[:, o:o + MLA_NOPE] = qf[:, o:o + MLA_NOPE].astype(q_ref.dtype)
        q_ref[:, o + MLA_NOPE:o + MLA_QK] = _rope(qf[:, o + MLA_NOPE:o + MLA_QK], c, sa, sb).astype(q_ref.dtype)
        k_ref[:, o:o + MLA_NOPE] = kvf[:, o:o + MLA_NOPE].astype(k_ref.dtype)
        k_ref[:, o + MLA_NOPE:o + MLA_QK] = k_pe
        v_ref[:, h * MLA_V:(h + 1) * MLA_V] = kvf[:, o + MLA_NOPE:o + MLA_QK].astype(v_ref.dtype)


def mla_prep(proj_small, rope_c, rope_sa, rope_sb, q_norm, kv_norm, wq, wkv, tm=256):
    t = proj_small.shape[0]
    hq = MLA_HEADS * MLA_QK
    hv = MLA_HEADS * MLA_V
    row = lambda i: (i, 0)
    fixed = lambda i: (0, 0)
    return pl.pallas_call(
        _mla_prep_kernel,
        grid=(t // tm,),
        in_specs=[pl.BlockSpec((tm, SM_WIDTH), row),
                  pl.BlockSpec((tm, LANES), row), pl.BlockSpec((tm, LANES), row), pl.BlockSpec((tm, LANES), row),
                  pl.BlockSpec((1, MLA_Q_RANK), fixed), pl.BlockSpec((1, MLA_KV_RANK), fixed),
                  pl.BlockSpec((MLA_Q_RANK, hq), fixed), pl.BlockSpec((MLA_KV_RANK, hq), fixed)],
        out_specs=[pl.BlockSpec((tm, hq), row), pl.BlockSpec((tm, hq), row), pl.BlockSpec((tm, hv), row)],
        out_shape=[jax.ShapeDtypeStruct((t, hq), BF16), jax.ShapeDtypeStruct((t, hq), BF16),
                   jax.ShapeDtypeStruct((t, hv), BF16)],
        compiler_params=_params("parallel"),
        name="mla_prep",
    )(proj_small, rope_c, rope_sa, rope_sb, q_norm.reshape(1, -1), kv_norm.reshape(1, -1), wq, wkv)


def _mla_attn_kernel(q_ref, k_ref, v_ref, o_ref, *, tq):
    qi = pl.program_id(2)
    scale = (MLA_NOPE + MLA_ROPE) ** -0.5
    q = q_ref[...]

    def update(carry, s, vj):
        m, l, acc = carry
        m_new = jnp.maximum(m, jnp.max(s, axis=-1, keepdims=True))
        alpha = jnp.exp(m - m_new)
        p = jnp.exp(s - m_new)
        l = alpha * l + jnp.sum(p, axis=-1, keepdims=True)
        acc = alpha * acc + _dot(p.astype(BF16), vj)
        return m_new, l, acc

    def body(j, carry):
        start = pl.multiple_of(j * tq, tq)
        s = _dot_nt(q, k_ref[pl.ds(start, tq), :]) * scale
        return update(carry, s, v_ref[pl.ds(start, tq), :])

    init = (jnp.full((tq, 1), -jnp.inf, F32), jnp.zeros((tq, 1), F32), jnp.zeros((tq, MLA_V), F32))
    carry = lax.fori_loop(0, qi, body, init)
    start = pl.multiple_of(qi * tq, tq)
    s = _dot_nt(q, k_ref[pl.ds(start, tq), :]) * scale
    chunk_bits = CHUNK.bit_length() - 1
    rc = lax.shift_right_logical(lax.broadcasted_iota(jnp.int32, (tq, tq), 0), chunk_bits)
    cc = lax.shift_right_logical(lax.broadcasted_iota(jnp.int32, (tq, tq), 1), chunk_bits)
    s = jnp.where(cc <= rc, s, -jnp.inf)
    _, l, acc = update(carry, s, v_ref[pl.ds(start, tq), :])
    o_ref[...] = (acc / l).astype(o_ref.dtype)


def mla_attention(q, k, v, batch, seq, tq=256):
    t = batch * seq
    nq = seq // tq
    return pl.pallas_call(
        functools.partial(_mla_attn_kernel, tq=tq),
        grid=(batch, MLA_HEADS, nq),
        in_specs=[pl.BlockSpec((tq, MLA_QK), lambda b, h, i: (b * nq + i, h)),
                  pl.BlockSpec((seq, MLA_QK), lambda b, h, i: (b, h)),
                  pl.BlockSpec((seq, MLA_V), lambda b, h, i: (b, h))],
        out_specs=pl.BlockSpec((tq, MLA_V), lambda b, h, i: (b * nq + i, h)),
        out_shape=jax.ShapeDtypeStruct((t, MLA_HEADS * MLA_V), BF16),
        compiler_params=_params("parallel", "parallel", "parallel"),
        name="mla_attention",
    )(q, k, v)


def _xattn_kernel(q_ref, k_ref, v_ref, o_ref):
    scale = XA_HEAD_DIM ** -0.5
    for h in range(XA_HEADS):
        sl = slice(h * XA_HEAD_DIM, (h + 1) * XA_HEAD_DIM)
        s = _dot_nt(q_ref[:, sl], k_ref[:, sl]) * scale
        p = jnp.exp(s - jnp.max(s, axis=-1, keepdims=True))
        l = jnp.sum(p, axis=-1, keepdims=True)
        o_ref[:, sl] = (_dot(p.astype(BF16), v_ref[:, sl]) / l).astype(o_ref.dtype)


def cross_attention(q, k, v, batch, seq, n_mem, tq=512):
    t, d = q.shape
    nq = seq // tq
    return pl.pallas_call(
        _xattn_kernel,
        grid=(batch, nq),
        in_specs=[pl.BlockSpec((tq, d), lambda b, i: (b * nq + i, 0)),
                  pl.BlockSpec((n_mem, d), lambda b, i: (b, 0)),
                  pl.BlockSpec((n_mem, d), lambda b, i: (b, 0))],
        out_specs=pl.BlockSpec((tq, d), lambda b, i: (b * nq + i, 0)),
        out_shape=jax.ShapeDtypeStruct((t, d), BF16),
        compiler_params=_params("parallel", "parallel"),
        name="cross_attention",
    )(q, k, v)


def _ffn_up_kernel(a_ref, wg_ref, wu_ref, cg_ref, cu_ref, bg_ref, bu_ref, o_ref, eg_ref, eu_ref, *, tiles_per_seq):
    i = pl.program_id(1)
    tm = a_ref.shape[0]

    @pl.when(i % tiles_per_seq == 0)
    def _():
        eg_ref[0:SUBLANES, :] = jnp.zeros((SUBLANES, eg_ref.shape[1]), F32)
        eu_ref[0:SUBLANES, :] = jnp.zeros((SUBLANES, eu_ref.shape[1]), F32)

    a = a_ref[...]

    def conv(w_ref, cw_ref, b_ref, e_ref):
        acc = _dot(a, w_ref[...])
        e_ref[SUBLANES:SUBLANES + tm, :] = acc
        cw = cw_ref[...]
        y = cw[FFN_CONV - 1:FFN_CONV, :] * acc + b_ref[...]
        for j in range(FFN_CONV - 1):
            shift = FFN_CONV - 1 - j
            y = y + cw[j:j + 1, :] * e_ref[SUBLANES - shift:SUBLANES - shift + tm, :]
        e_ref[0:SUBLANES, :] = acc[tm - SUBLANES:tm, :]
        return y

    gate = conv(wg_ref, cg_ref, bg_ref, eg_ref)
    up = conv(wu_ref, cu_ref, bu_ref, eu_ref)
    o_ref[...] = (_silu(gate) * up).astype(o_ref.dtype)


def ffn_up(a, w_up, conv_w, conv_b, seq, tm=1024, tn=512):
    m, k = a.shape
    nb = D_FF // tn
    return pl.pallas_call(
        functools.partial(_ffn_up_kernel, tiles_per_seq=seq // tm),
        grid=(nb, m // tm),
        in_specs=[pl.BlockSpec((tm, k), lambda j, i: (i, 0)),
                  pl.BlockSpec((k, tn), lambda j, i: (0, j)),
                  pl.BlockSpec((k, tn), lambda j, i: (0, j + nb)),
                  pl.BlockSpec((FFN_CONV, tn), lambda j, i: (0, j)),
                  pl.BlockSpec((FFN_CONV, tn), lambda j, i: (0, j + nb)),
                  pl.BlockSpec((1, tn), lambda j, i: (0, j)),
                  pl.BlockSpec((1, tn), lambda j, i: (0, j + nb))],
        out_specs=pl.BlockSpec((tm, tn), lambda j, i: (i, j)),
        out_shape=jax.ShapeDtypeStruct((m, D_FF), BF16),
        scratch_shapes=[pltpu.VMEM((SUBLANES + tm, tn), F32), pltpu.VMEM((SUBLANES + tm, tn), F32)],
        compiler_params=_params("parallel", "arbitrary"),
        name="ffn_up",
    )(a, w_up, w_up, conv_w, conv_w, conv_b.reshape(1, -1), conv_b.reshape(1, -1))


def _prep_in_small(w_in_l):
    d = w_in_l.shape[0]
    base = 4 * DN_WIDTH
    wb = w_in_l[:, base:base + DN_HEADS]
    wa = w_in_l[:, base + DN_HEADS:base + 2 * DN_HEADS]
    mq0 = base + 2 * DN_HEADS
    wq = w_in_l[:, mq0:mq0 + MLA_Q_RANK]
    wckv = w_in_l[:, mq0 + MLA_Q_RANK:mq0 + MLA_Q_RANK + MLA_KV_RANK]
    wkpe = w_in_l[:, mq0 + MLA_Q_RANK + MLA_KV_RANK:mq0 + MLA_Q_RANK + MLA_KV_RANK + MLA_ROPE]
    z = lambda n: jnp.zeros((d, n), w_in_l.dtype)
    return jnp.concatenate([wq, wckv, wkpe, z(LANES - MLA_ROPE), wb, wa, z(LANES - 2 * DN_HEADS)], axis=1)


def _prep_wqb(w):
    r = w.shape[0]
    w3 = w.reshape(r, MLA_HEADS, MLA_NOPE + MLA_ROPE)
    w3 = jnp.concatenate([w3, jnp.zeros((r, MLA_HEADS, LANES - MLA_ROPE), w.dtype)], axis=2)
    return w3.reshape(r, MLA_HEADS * MLA_QK)


def kernel(x, mem, positions, norm_mix, w_in, dn_conv, dn_a_log, dn_dt_bias, dn_out_norm, mla_q_norm,
           mla_w_qb, mla_kv_norm, mla_w_kvb, w_out, mem_norm, norm_xattn, xa_wq, xa_wk, xa_wv, xa_wo,
           norm_ffn, ffn_w_up, ffn_conv, ffn_conv_bias, ffn_w_down, norm_final):
    batch, seq, d = x.shape
    n_mem = mem.shape[1]
    t = batch * seq
    depth = w_in.shape[0]

    rope_c, rope_sa, rope_sb = rope_tables(positions)
    mem_n = rmsnorm_rows(mem.reshape(batch * n_mem, d), mem_norm, BF16, tm=256)
    h = x.reshape(t, d)
    u = rmsnorm_rows(h, norm_mix[0], BF16)

    for l in range(depth):
        w_main = w_in[l, :, 0:4 * DN_WIDTH].astype(BF16)
        w_small = _prep_in_small(w_in[l]).astype(BF16)
        proj_main = matmul(u, w_main, F32, tm=1024, tn=1024, name="in_proj_main")
        proj_small = matmul(u, w_small, F32, tm=1024, tn=SM_WIDTH, name="in_proj_small")

        o_dn = deltanet(proj_main, proj_small, dn_conv[l], dn_a_log[l], dn_dt_bias[l], dn_out_norm[l], batch, seq)

        q, k, v = mla_prep(proj_small, rope_c, rope_sa, rope_sb, mla_q_norm[l], mla_kv_norm[l],
                           _prep_wqb(mla_w_qb[l]).astype(BF16), mla_w_kvb[l].astype(BF16))
        o_mla = mla_attention(q, k, v, batch, seq)

        h, u = matmul_residual_norm([o_dn, o_mla], w_out[l].astype(BF16), h, norm_xattn[l], BF16, tm=256,
                                    name="out_proj")

        xq = matmul(u, xa_wq[l].astype(BF16), BF16, tm=1024, tn=1024, name="xa_q")
        xk = matmul(mem_n, xa_wk[l].astype(BF16), BF16, tm=1024, tn=1024, name="xa_k")
        xv = matmul(mem_n, xa_wv[l].astype(BF16), BF16, tm=1024, tn=1024, name="xa_v")
        o_xa = cross_attention(xq, xk, xv, batch, seq, n_mem)
        h, u = matmul_residual_norm([o_xa], xa_wo[l].astype(BF16), h, norm_ffn[l], BF16, tm=256, name="xa_o")

        hid = ffn_up(u, ffn_w_up[l].astype(BF16), ffn_conv[l], ffn_conv_bias[l], seq)
        w_down = ffn_w_down[l].astype(BF16)
        if l + 1 < depth:
            h, u = matmul_residual_norm([hid], w_down, h, norm_mix[l + 1], BF16, tm=512, tk=D_FF // 4,
                                        name="ffn_down")
        else:
            out = matmul_residual_norm([hid], w_down, h, norm_final, x.dtype, tm=512, tk=D_FF // 4,
                                       emit_h=False, name="ffn_down_final")
    return out.reshape(batch, seq, d)
```

```python
import functools

import numpy as np
import jax
import jax.numpy as jnp
from jax import lax
from jax.experimental import pallas as pl
from jax.experimental.pallas import tpu as pltpu

F32 = jnp.float32
BF16 = jnp.bfloat16

D_MODEL = 2048
DEPTH = 4
CHUNK = 64
DN_HEADS = 8
DN_HEAD_DIM = 128
DN_WIDTH = DN_HEADS * DN_HEAD_DIM
DN_CONV = 4
MLA_HEADS = 8
MLA_NOPE = 128
MLA_ROPE = 64
MLA_V = 128
MLA_Q_RANK = 512
MLA_KV_RANK = 256
ROPE_BASE = 10000.0
XA_HEADS = 4
XA_HEAD_DIM = D_MODEL // XA_HEADS
D_FF = 5632
FFN_CONV = 3
EPS = 1e-6

LANES = 128
SUBLANES = 8
VMEM_LIMIT = 56 * 1024 * 1024

SM_Q = 0
SM_CKV = MLA_Q_RANK
SM_KPE = SM_CKV + MLA_KV_RANK
SM_BA = SM_KPE + LANES
SM_WIDTH = SM_BA + LANES
MLA_QK = MLA_NOPE + LANES
MLA_TILE = 256


def _params(*sem):
    return pltpu.CompilerParams(dimension_semantics=sem, vmem_limit_bytes=VMEM_LIMIT)


def _dot(a, b):
    return jnp.dot(a, b, preferred_element_type=F32)


def _dot_nt(a, b):
    return lax.dot_general(a, b, (((1,), (1,)), ((), ())), preferred_element_type=F32)


def _bdot(a, b):
    return jnp.einsum("hmk,hkn->hmn", a, b, preferred_element_type=F32)


def _bdot_nt(a, b):
    return jnp.einsum("hmk,hnk->hmn", a, b, preferred_element_type=F32)


def _split_bf16(a):
    hi = a.astype(BF16)
    return hi, (a - hi.astype(F32)).astype(BF16)


def _rms(x, gain):
    return x * lax.rsqrt(jnp.mean(x * x, axis=-1, keepdims=True) + EPS) * gain


def _silu(x):
    return x * jax.nn.sigmoid(x)


def _rmsnorm_kernel(x_ref, g_ref, o_ref):
    o_ref[...] = _rms(x_ref[...].astype(F32), g_ref[...]).astype(o_ref.dtype)


def rmsnorm_rows(x, gain, out_dtype, tm=512):
    m, d = x.shape
    return pl.pallas_call(
        _rmsnorm_kernel,
        grid=(m // tm,),
        in_specs=[pl.BlockSpec((tm, d), lambda i: (i, 0)),
                  pl.BlockSpec((1, d), lambda i: (0, 0))],
        out_specs=pl.BlockSpec((tm, d), lambda i: (i, 0)),
        out_shape=jax.ShapeDtypeStruct((m, d), out_dtype),
        compiler_params=_params("parallel"),
        name="rmsnorm_rows",
    )(x, gain.reshape(1, d))


def _mm_kernel(a_ref, w_ref, o_ref):
    o_ref[...] = _dot(a_ref[...], w_ref[...]).astype(o_ref.dtype)


def matmul(a, w, out_dtype, tm, tn, name="matmul"):
    m, k = a.shape
    n = w.shape[1]
    return pl.pallas_call(
        _mm_kernel,
        grid=(n // tn, m // tm),
        in_specs=[pl.BlockSpec((tm, k), lambda j, i: (i, 0)),
                  pl.BlockSpec((k, tn), lambda j, i: (0, j))],
        out_specs=pl.BlockSpec((tm, tn), lambda j, i: (i, j)),
        out_shape=jax.ShapeDtypeStruct((m, n), out_dtype),
        compiler_params=_params("parallel", "parallel"),
        name=name,
    )(a, w)


def _mm_res_norm_kernel(*refs, n_parts, nk, emit_h):
    a_refs = refs[:n_parts]
    w_ref, h_ref, g_ref = refs[n_parts:n_parts + 3]
    outs = refs[n_parts + 3:]
    if emit_h:
        ho_ref, uo_ref = outs[0], outs[1]
        rest = outs[2:]
    else:
        ho_ref, uo_ref = None, outs[0]
        rest = outs[1:]

    def finish(acc):
        h = h_ref[...] + acc
        if emit_h:
            ho_ref[...] = h
        uo_ref[...] = _rms(h, g_ref[...]).astype(uo_ref.dtype)

    if nk == 1:
        off = 0
        acc = None
        for a_ref in a_refs:
            kp = a_ref.shape[1]
            part = _dot(a_ref[...], w_ref[off:off + kp, :])
            acc = part if acc is None else acc + part
            off += kp
        finish(acc)
    else:
        acc_ref = rest[0]
        kk = pl.program_id(1)

        @pl.when(kk == 0)
        def _():
            acc_ref[...] = jnp.zeros_like(acc_ref)

        acc_ref[...] += _dot(a_refs[0][...], w_ref[...])

        @pl.when(kk == nk - 1)
        def _():
            finish(acc_ref[...])


def matmul_residual_norm(a_parts, w, h, gain, u_dtype, tm, tk=None, emit_h=True, name="mm_res_norm"):
    m, n = h.shape
    k = w.shape[0]
    n_parts = len(a_parts)
    if tk is None:
        tk = k
    nk = k // tk
    assert nk == 1 or n_parts == 1
    if nk == 1:
        a_specs = [pl.BlockSpec((tm, a.shape[1]), lambda i, kk: (i, 0)) for a in a_parts]
    else:
        a_specs = [pl.BlockSpec((tm, tk), lambda i, kk: (i, kk))]
    in_specs = a_specs + [
        pl.BlockSpec((tk, n), lambda i, kk: (kk, 0)),
        pl.BlockSpec((tm, n), lambda i, kk: (i, 0)),
        pl.BlockSpec((1, n), lambda i, kk: (0, 0)),
    ]
    out_specs = [pl.BlockSpec((tm, n), lambda i, kk: (i, 0))]
    out_shape = [jax.ShapeDtypeStruct((m, n), u_dtype)]
    if emit_h:
        out_specs = [pl.BlockSpec((tm, n), lambda i, kk: (i, 0))] + out_specs
        out_shape = [jax.ShapeDtypeStruct((m, n), F32)] + out_shape
    scratch = [] if nk == 1 else [pltpu.VMEM((tm, n), F32)]
    res = pl.pallas_call(
        functools.partial(_mm_res_norm_kernel, n_parts=n_parts, nk=nk, emit_h=emit_h),
        grid=(m // tm, nk),
        in_specs=in_specs,
        out_specs=out_specs,
        out_shape=out_shape,
        scratch_shapes=scratch,
        compiler_params=_params("parallel", "arbitrary"),
        name=name,
    )(*a_parts, w, h, gain.reshape(1, n))
    return tuple(res) if emit_h else res[0]


def _rope_table_kernel(pos_ref, inv_ref, c_ref, sa_ref, sb_ref):
    ang = pos_ref[...].astype(F32) * inv_ref[...]
    c = jnp.cos(ang)
    s = jnp.sin(ang)
    lane = lax.broadcasted_iota(jnp.int32, ang.shape, 1)
    half = MLA_ROPE // 2
    c_ref[...] = jnp.where(lane < MLA_ROPE, c, 0.0)
    sa_ref[...] = jnp.where(lane < half, -s, 0.0)
    sb_ref[...] = jnp.where((lane >= half) & (lane < MLA_ROPE), s, 0.0)


def rope_tables(positions, tm=512):
    t = positions.size
    inv = ROPE_BASE ** (-jnp.arange(0, MLA_ROPE, 2, dtype=F32) / MLA_ROPE)
    inv_row = jnp.concatenate([inv, inv, jnp.zeros((LANES - MLA_ROPE,), F32)]).reshape(1, LANES)
    spec = pl.BlockSpec((tm, LANES), lambda i: (i, 0))
    shape = jax.ShapeDtypeStruct((t, LANES), F32)
    return pl.pallas_call(
        _rope_table_kernel,
        grid=(t // tm,),
        in_specs=[pl.BlockSpec((tm, 1), lambda i: (i, 0)),
                  pl.BlockSpec((1, LANES), lambda i: (0, 0))],
        out_specs=[spec, spec, spec],
        out_shape=[shape, shape, shape],
        compiler_params=_params("parallel"),
        name="rope_tables",
    )(positions.reshape(t, 1), inv_row)


def _rope(x, c, sa, sb):
    half = MLA_ROPE // 2
    return x * c + pltpu.roll(x, LANES - half, 1) * sa + pltpu.roll(x, half, 1) * sb


def _deltanet_kernel(x_ref, ba_ref, cw_ref, alog_ref, dtb_ref, onorm_ref, o_ref, xs_ref, state_ref):
    c_idx = pl.program_id(1)
    C, D, H = CHUNK, DN_HEAD_DIM, DN_HEADS
    W3 = 3 * DN_WIDTH

    @pl.when(c_idx == 0)
    def _():
        xs_ref[0:SUBLANES, :] = jnp.zeros((SUBLANES, W3), F32)
        state_ref[...] = jnp.zeros_like(state_ref)

    x = x_ref[:, 0:W3]
    xs_ref[SUBLANES:SUBLANES + C, :] = x
    cw = cw_ref[...]
    y = cw[DN_CONV - 1:DN_CONV, :] * x
    for j in range(DN_CONV - 1):
        shift = DN_CONV - 1 - j
        y = y + cw[j:j + 1, :] * xs_ref[SUBLANES - shift:SUBLANES - shift + C, :]
    xs_ref[0:SUBLANES, :] = x[C - SUBLANES:C, :]
    y = _silu(y)

    ba = ba_ref[...]
    beta_all = jax.nn.sigmoid(ba)
    sp_in = ba + dtb_ref[...]
    softplus = jnp.maximum(sp_in, 0.0) + jnp.log1p(jnp.exp(-jnp.abs(sp_in)))
    g_all = -jnp.exp(alog_ref[...]) * softplus
    row = lax.broadcasted_iota(jnp.int32, (C, LANES), 0)
    G = g_all
    step = 1
    while step < C:
        G = G + jnp.where(row >= step, pltpu.roll(G, step, 0), 0.0)
        step *= 2
    GT = G.T
    g_last = G[C - 1:C, :]
    e_g = jnp.exp(G)
    e_rem = jnp.exp(g_last - G)
    e_last = jnp.exp(g_last)

    ri = lax.broadcasted_iota(jnp.int32, (C, C), 0)
    ci = lax.broadcasted_iota(jnp.int32, (C, C), 1)
    incl = (ri >= ci)[None]
    strict = (ri > ci)[None]
    eye = jnp.where(ri == ci, 1.0, 0.0).astype(F32)[None]
    onorm = onorm_ref[...]

    def heads(off):
        return jnp.stack([y[:, off + h * D:off + (h + 1) * D] for h in range(H)])

    def head_cols(a, lane0):
        return jnp.stack([a[:, lane0 + h:lane0 + h + 1] for h in range(H)])

    q = heads(0)
    k = heads(DN_WIDTH)
    v = heads(2 * DN_WIDTH)
    q = q * lax.rsqrt(jnp.sum(q * q, axis=-1, keepdims=True) + EPS) * (D ** -0.5)
    k = k * lax.rsqrt(jnp.sum(k * k, axis=-1, keepdims=True) + EPS)
    beta = head_cols(beta_all, 0)
    g_col = head_cols(G, H)
    g_row = jnp.stack([GT[H + h:H + h + 1, :] for h in range(H)])
    eg_col = head_cols(e_g, H)
    erem_col = head_cols(e_rem, H)
    elast = head_cols(e_last, H)
    decay = jnp.exp(jnp.where(incl, g_col - g_row, -jnp.inf))
    kb = k * beta
    k16 = k.astype(BF16)
    kq = _bdot_nt(jnp.concatenate([kb.astype(BF16), q.astype(BF16)], axis=1), k16)
    lower = jnp.where(strict, kq[:, 0:C] * decay, 0.0)
    attn = kq[:, C:2 * C] * decay

    lh, ll = _split_bf16(lower)
    r = _bdot(jnp.concatenate([lh, ll], axis=1), lh)
    p = r[:, 0:C] + r[:, C:2 * C] + _bdot(lh, ll)
    t_inv = eye - lower
    sq = 2
    while True:
        th, tl = _split_bf16(t_inv)
        ph, pl_ = _split_bf16(p)
        last = sq * 2 >= C
        if last:
            r1 = _bdot(jnp.concatenate([th, tl], axis=1), ph)
            t_inv = t_inv + (r1[:, 0:C] + r1[:, C:2 * C] + _bdot(th, pl_))
            break
        r1 = _bdot(jnp.concatenate([th, tl, ph, pl_], axis=1), ph)
        r2 = _bdot(jnp.concatenate([th, ph], axis=1), pl_)
        t_inv = t_inv + (r1[:, 0:C] + r1[:, C:2 * C] + r2[:, 0:C])
        p = r1[:, 2 * C:3 * C] + r1[:, 3 * C:4 * C] + r2[:, C:2 * C]
        sq *= 2

    rhs = jnp.concatenate([v * beta, kb * eg_col], axis=2)
    sol = _bdot(t_inv.astype(BF16), rhs.astype(BF16))
    u = sol[:, :, 0:D]
    w = sol[:, :, D:2 * D]
    s_old = state_ref[...]
    s16 = s_old.astype(BF16)
    ws = _bdot(jnp.concatenate([w.astype(BF16), (q * eg_col).astype(BF16)], axis=1), s16)
    v_new = u - ws[:, 0:C]
    vn16 = v_new.astype(BF16)
    o = ws[:, C:2 * C] + _bdot(attn.astype(BF16), vn16)
    k_dec = (k * erem_col).astype(BF16)
    state_ref[...] = s_old * elast + jnp.einsum("hck,hcv->hkv", k_dec, vn16, preferred_element_type=F32)
    on = _rms(o, onorm[None])
    for h in range(H):
        z = x_ref[:, W3 + h * D:W3 + (h + 1) * D]
        o_ref[:, h * D:(h + 1) * D] = (on[h] * _silu(z)).astype(o_ref.dtype)


def deltanet(proj_main, proj_small, conv_w, a_log, dt_bias, out_norm, batch, seq):
    t = batch * seq
    nc = seq // CHUNK
    pad = jnp.zeros((LANES - 2 * DN_HEADS,), F32)
    alog_row = jnp.concatenate([jnp.zeros((DN_HEADS,), F32), a_log, pad]).reshape(1, LANES)
    dtb_row = jnp.concatenate([jnp.zeros((DN_HEADS,), F32), dt_bias, pad]).reshape(1, LANES)
    w4 = 4 * DN_WIDTH
    return pl.pallas_call(
        _deltanet_kernel,
        grid=(batch, nc),
        in_specs=[pl.BlockSpec((CHUNK, w4), lambda b, c: (b * nc + c, 0)),
                  pl.BlockSpec((CHUNK, LANES), lambda b, c: (b * nc + c, SM_BA // LANES)),
                  pl.BlockSpec((DN_CONV, 3 * DN_WIDTH), lambda b, c: (0, 0)),
                  pl.BlockSpec((1, LANES), lambda b, c: (0, 0)),
                  pl.BlockSpec((1, LANES), lambda b, c: (0, 0)),
                  pl.BlockSpec((1, DN_HEAD_DIM), lambda b, c: (0, 0))],
        out_specs=pl.BlockSpec((CHUNK, DN_WIDTH), lambda b, c: (b * nc + c, 0)),
        out_shape=jax.ShapeDtypeStruct((t, DN_WIDTH), BF16),
        scratch_shapes=[pltpu.VMEM((SUBLANES + CHUNK, 3 * DN_WIDTH), F32),
                        pltpu.VMEM((DN_HEADS, DN_HEAD_DIM, DN_HEAD_DIM), F32)],
        compiler_params=_params("arbitrary", "arbitrary"),
        name="deltanet",
    )(proj_main, proj_small, conv_w, alog_row, dtb_row, out_norm.reshape(1, DN_HEAD_DIM))


def _mla_prep_kernel(x_ref, c_ref, sa_ref, sb_ref, qn_ref, kvn_ref, wq_ref, wkv_ref, qt_ref, k_ref, vt_ref):
    c, sa, sb = c_ref[...], sa_ref[...], sb_ref[...]
    scale = (MLA_NOPE + MLA_ROPE) ** -0.5
    q_lat = _rms(x_ref[:, SM_Q:SM_Q + MLA_Q_RANK], qn_ref[...])
    qf = _dot(q_lat.astype(BF16), wq_ref[...]) * scale
    c_kv = _rms(x_ref[:, SM_CKV:SM_CKV + MLA_KV_RANK], kvn_ref[...])
    kvf = _dot(c_kv.astype(BF16), wkv_ref[...])
    k_pe = _rope(x_ref[:, SM_KPE:SM_KPE + LANES], c, sa, sb).astype(k_ref.dtype)
    for h in range(MLA_HEADS):
        o = h * MLA_QK
        q_h = jnp.concatenate([qf[:, o:o + MLA_NOPE], _rope(qf[:, o + MLA_NOPE:o + MLA_QK], c, sa, sb)], axis=1)
        qt_ref[o:o + MLA_QK, :] = q_h.T.astype(qt_ref.dtype)
        k_ref[:, o:o + MLA_NOPE] = kvf[:, o:o + MLA_NOPE].astype(k_ref.dtype)
        k_ref[:, o + MLA_NOPE:o + MLA_QK] = k_pe
        vt_ref[h * MLA_V:(h + 1) * MLA_V, :] = kvf[:, o + MLA_NOPE:o + MLA_QK].T.astype(vt_ref.dtype)


def mla_prep(proj_small, rope_c, rope_sa, rope_sb, q_norm, kv_norm, wq, wkv, batch, seq, tm):
    t = proj_small.shape[0]
    ns = seq // tm
    hq = MLA_HEADS * MLA_QK
    hv = MLA_HEADS * MLA_V
    row = lambda i: (i, 0)
    fixed = lambda i: (0, 0)
    return pl.pallas_call(
        _mla_prep_kernel,
        grid=(t // tm,),
        in_specs=[pl.BlockSpec((tm, SM_WIDTH), row),
                  pl.BlockSpec((tm, LANES), row), pl.BlockSpec((tm, LANES), row), pl.BlockSpec((tm, LANES), row),
                  pl.BlockSpec((1, MLA_Q_RANK), fixed), pl.BlockSpec((1, MLA_KV_RANK), fixed),
                  pl.BlockSpec((MLA_Q_RANK, hq), fixed), pl.BlockSpec((MLA_KV_RANK, hq), fixed)],
        out_specs=[pl.BlockSpec((None, hq, tm), lambda i: (i // ns, 0, i % ns)),
                   pl.BlockSpec((tm, hq), row),
                   pl.BlockSpec((None, None, hv, tm), lambda i: (i // ns, i % ns, 0, 0))],
        out_shape=[jax.ShapeDtypeStruct((batch, hq, seq), BF16),
                   jax.ShapeDtypeStruct((t, hq), BF16),
                   jax.ShapeDtypeStruct((batch, ns, hv, tm), BF16)],
        compiler_params=_params("parallel"),
        name="mla_prep",
    )(proj_small, rope_c, rope_sa, rope_sb, q_norm.reshape(1, -1), kv_norm.reshape(1, -1), wq, wkv)


def _mla_attn_kernel(qt_ref, k_ref, vt_ref, o_ref, *, tq):
    qi = pl.program_id(1)
    H = MLA_HEADS
    qt = qt_ref[...].reshape(H, MLA_QK, tq)

    def scores(j):
        start = pl.multiple_of(j * tq, tq)
        kk = k_ref[pl.ds(start, tq), :]
        kh = jnp.stack([kk[:, h * MLA_QK:(h + 1) * MLA_QK] for h in range(H)])
        return _bdot(kh, qt)

    def update(carry, s, vtj):
        m, l, acc = carry
        m_new = jnp.maximum(m, jnp.max(s, axis=1, keepdims=True))
        alpha = jnp.exp(m - m_new)
        p = jnp.exp(s - m_new)
        l = alpha * l + jnp.sum(p, axis=1, keepdims=True)
        acc = alpha * acc + _bdot(vtj.reshape(H, MLA_V, tq), p.astype(BF16))
        return m_new, l, acc

    def body(j, carry):
        return update(carry, scores(j), vt_ref[j])

    init = (jnp.full((H, 1, tq), -jnp.inf, F32), jnp.zeros((H, 1, tq), F32), jnp.zeros((H, MLA_V, tq), F32))
    carry = lax.fori_loop(0, qi, body, init)
    chunk_bits = CHUNK.bit_length() - 1
    kc = lax.shift_right_logical(lax.broadcasted_iota(jnp.int32, (tq, tq), 0), chunk_bits)
    qc = lax.shift_right_logical(lax.broadcasted_iota(jnp.int32, (tq, tq), 1), chunk_bits)
    s = jnp.where((kc <= qc)[None], scores(qi), -jnp.inf)
    _, l, acc = update(carry, s, vt_ref[qi])
    o = acc / l
    for h in range(H):
        o_ref[:, h * MLA_V:(h + 1) * MLA_V] = o[h].T.astype(o_ref.dtype)


def mla_attention(qt, k, vt, batch, seq, tq):
    t = batch * seq
    nq = seq // tq
    hq = MLA_HEADS * MLA_QK
    hv = MLA_HEADS * MLA_V
    return pl.pallas_call(
        functools.partial(_mla_attn_kernel, tq=tq),
        grid=(batch, nq),
        in_specs=[pl.BlockSpec((None, hq, tq), lambda b, i: (b, 0, i)),
                  pl.BlockSpec((seq, hq), lambda b, i: (b, 0)),
                  pl.BlockSpec((None, nq, hv, tq), lambda b, i: (b, 0, 0, 0))],
        out_specs=pl.BlockSpec((tq, hv), lambda b, i: (b * nq + i, 0)),
        out_shape=jax.ShapeDtypeStruct((t, hv), BF16),
        compiler_params=_params("parallel", "parallel"),
        name="mla_attention",
    )(qt, k, vt)


def _xattn_kernel(q_ref, k_ref, v_ref, o_ref):
    scale = XA_HEAD_DIM ** -0.5
    for h in range(XA_HEADS):
        sl = slice(h * XA_HEAD_DIM, (h + 1) * XA_HEAD_DIM)
        s = _dot_nt(q_ref[:, sl], k_ref[:, sl]) * scale
        p = jnp.exp(s - jnp.max(s, axis=-1, keepdims=True))
        l = jnp.sum(p, axis=-1, keepdims=True)
        o_ref[:, sl] = (_dot(p.astype(BF16), v_ref[:, sl]) / l).astype(o_ref.dtype)


def cross_attention(q, k, v, batch, seq, n_mem, tq=512):
    t, d = q.shape
    nq = seq // tq
    return pl.pallas_call(
        _xattn_kernel,
        grid=(batch, nq),
        in_specs=[pl.BlockSpec((tq, d), lambda b, i: (b * nq + i, 0)),
                  pl.BlockSpec((n_mem, d), lambda b, i: (b, 0)),
                  pl.BlockSpec((n_mem, d), lambda b, i: (b, 0))],
        out_specs=pl.BlockSpec((tq, d), lambda b, i: (b * nq + i, 0)),
        out_shape=jax.ShapeDtypeStruct((t, d), BF16),
        compiler_params=_params("parallel", "parallel"),
        name="cross_attention",
    )(q, k, v)


def _ffn_up_kernel(a_ref, wg_ref, wu_ref, cg_ref, cu_ref, bg_ref, bu_ref, o_ref, eg_ref, eu_ref, *, tiles_per_seq):
    i = pl.program_id(1)
    tm = a_ref.shape[0]

    @pl.when(i % tiles_per_seq == 0)
    def _():
        eg_ref[0:SUBLANES, :] = jnp.zeros((SUBLANES, eg_ref.shape[1]), F32)
        eu_ref[0:SUBLANES, :] = jnp.zeros((SUBLANES, eu_ref.shape[1]), F32)

    a = a_ref[...]

    def conv(w_ref, cw_ref, b_ref, e_ref):
        acc = _dot(a, w_ref[...])
        e_ref[SUBLANES:SUBLANES + tm, :] = acc
        cw = cw_ref[...]
        y = cw[FFN_CONV - 1:FFN_CONV, :] * acc + b_ref[...]
        for j in range(FFN_CONV - 1):
            shift = FFN_CONV - 1 - j
            y = y + cw[j:j + 1, :] * e_ref[SUBLANES - shift:SUBLANES - shift + tm, :]
        e_ref[0:SUBLANES, :] = acc[tm - SUBLANES:tm, :]
        return y

    gate = conv(wg_ref, cg_ref, bg_ref, eg_ref)
    up = conv(wu_ref, cu_ref, bu_ref, eu_ref)
    o_ref[...] = (_silu(gate) * up).astype(o_ref.dtype)


def ffn_up(a, w_up, conv_w, conv_b, seq, tm=1024, tn=512):
    m, k = a.shape
    nb = D_FF // tn
    return pl.pallas_call(
        functools.partial(_ffn_up_kernel, tiles_per_seq=seq // tm),
        grid=(nb, m // tm),
        in_specs=[pl.BlockSpec((tm, k), lambda j, i: (i, 0)),
                  pl.BlockSpec((k, tn), lambda j, i: (0, j)),
                  pl.BlockSpec((k, tn), lambda j, i: (0, j + nb)),
                  pl.BlockSpec((FFN_CONV, tn), lambda j, i: (0, j)),
                  pl.BlockSpec((FFN_CONV, tn), lambda j, i: (0, j + nb)),
                  pl.BlockSpec((1, tn), lambda j, i: (0, j)),
                  pl.BlockSpec((1, tn), lambda j, i: (0, j + nb))],
        out_specs=pl.BlockSpec((tm, tn), lambda j, i: (i, j)),
        out_shape=jax.ShapeDtypeStruct((m, D_FF), BF16),
        scratch_shapes=[pltpu.VMEM((SUBLANES + tm, tn), F32), pltpu.VMEM((SUBLANES + tm, tn), F32)],
        compiler_params=_params("parallel", "arbitrary"),
        name="ffn_up",
    )(a, w_up, w_up, conv_w, conv_w, conv_b.reshape(1, -1), conv_b.reshape(1, -1))


def _prep_in_small(w_in_l):
    d = w_in_l.shape[0]
    base = 4 * DN_WIDTH
    wb = w_in_l[:, base:base + DN_HEADS]
    wa = w_in_l[:, base + DN_HEADS:base + 2 * DN_HEADS]
    mq0 = base + 2 * DN_HEADS
    wq = w_in_l[:, mq0:mq0 + MLA_Q_RANK]
    wckv = w_in_l[:, mq0 + MLA_Q_RANK:mq0 + MLA_Q_RANK + MLA_KV_RANK]
    wkpe = w_in_l[:, mq0 + MLA_Q_RANK + MLA_KV_RANK:mq0 + MLA_Q_RANK + MLA_KV_RANK + MLA_ROPE]
    z = lambda n: jnp.zeros((d, n), w_in_l.dtype)
    return jnp.concatenate([wq, wckv, wkpe, z(LANES - MLA_ROPE), wb, wa, z(LANES - 2 * DN_HEADS)], axis=1)


def _prep_wqb(w):
    r = w.shape[0]
    w3 = w.reshape(r, MLA_HEADS, MLA_NOPE + MLA_ROPE)
    w3 = jnp.concatenate([w3, jnp.zeros((r, MLA_HEADS, LANES - MLA_ROPE), w.dtype)], axis=2)
    return w3.reshape(r, MLA_HEADS * MLA_QK)


def kernel(x, mem, positions, norm_mix, w_in, dn_conv, dn_a_log, dn_dt_bias, dn_out_norm, mla_q_norm,
           mla_w_qb, mla_kv_norm, mla_w_kvb, w_out, mem_norm, norm_xattn, xa_wq, xa_wk, xa_wv, xa_wo,
           norm_ffn, ffn_w_up, ffn_conv, ffn_conv_bias, ffn_w_down, norm_final):
    batch, seq, d = x.shape
    n_mem = mem.shape[1]
    t = batch * seq
    depth = w_in.shape[0]

    rope_c, rope_sa, rope_sb = rope_tables(positions)
    mem_n = rmsnorm_rows(mem.reshape(batch * n_mem, d), mem_norm, BF16, tm=256)
    h = x.reshape(t, d)
    u = rmsnorm_rows(h, norm_mix[0], BF16)

    for l in range(depth):
        w_main = w_in[l, :, 0:4 * DN_WIDTH].astype(BF16)
        w_small = _prep_in_small(w_in[l]).astype(BF16)
        proj_main = matmul(u, w_main, F32, tm=1024, tn=1024, name="in_proj_main")
        proj_small = matmul(u, w_small, F32, tm=1024, tn=SM_WIDTH, name="in_proj_small")

        o_dn = deltanet(proj_main, proj_small, dn_conv[l], dn_a_log[l], dn_dt_bias[l], dn_out_norm[l], batch, seq)

        qt, k, vt = mla_prep(proj_small, rope_c, rope_sa, rope_sb, mla_q_norm[l], mla_kv_norm[l],
                             _prep_wqb(mla_w_qb[l]).astype(BF16), mla_w_kvb[l].astype(BF16),
                             batch, seq, tm=MLA_TILE)
        o_mla = mla_attention(qt, k, vt, batch, seq, tq=MLA_TILE)

        h, u = matmul_residual_norm([o_dn, o_mla], w_out[l].astype(BF16), h, norm_xattn[l], BF16, tm=256,
                                    name="out_proj")

        xq = matmul(u, xa_wq[l].astype(BF16), BF16, tm=1024, tn=1024, name="xa_q")
        xk = matmul(mem_n, xa_wk[l].astype(BF16), BF16, tm=1024, tn=1024, name="xa_k")
        xv = matmul(mem_n, xa_wv[l].astype(BF16), BF16, tm=1024, tn=1024, name="xa_v")
        o_xa = cross_attention(xq, xk, xv, batch, seq, n_mem)
        h, u = matmul_residual_norm([o_xa], xa_wo[l].astype(BF16), h, norm_ffn[l], BF16, tm=256, name="xa_o")

        hid = ffn_up(u, ffn_w_up[l].astype(BF16), ffn_conv[l], ffn_conv_bias[l], seq)
        w_down = ffn_w_down[l].astype(BF16)
        if l + 1 < depth:
            h, u = matmul_residual_norm([hid], w_down, h, norm_mix[l + 1], BF16, tm=512, tk=D_FF // 4,
                                        name="ffn_down")
        else:
            out = matmul_residual_norm([hid], w_down, h, norm_final, x.dtype, tm=512, tk=D_FF // 4,
                                       emit_h=False, name="ffn_down_final")
    return out.reshape(batch, seq, d)
```

```python
import functools

import numpy as np
import jax
import jax.numpy as jnp
from jax import lax
from jax.experimental import pallas as pl
from jax.experimental.pallas import tpu as pltpu

F32 = jnp.float32
BF16 = jnp.bfloat16

D_MODEL = 2048
DEPTH = 4
CHUNK = 64
DN_HEADS = 8
DN_HEAD_DIM = 128
DN_WIDTH = DN_HEADS * DN_HEAD_DIM
DN_CONV = 4
MLA_HEADS = 8
MLA_NOPE = 128
MLA_ROPE = 64
MLA_V = 128
MLA_Q_RANK = 512
MLA_KV_RANK = 256
ROPE_BASE = 10000.0
XA_HEADS = 4
XA_HEAD_DIM = D_MODEL // XA_HEADS
D_FF = 5632
FFN_CONV = 3
EPS = 1e-6

LANES = 128
SUBLANES = 8
VMEM_LIMIT = 56 * 1024 * 1024

SM_Q = 0
SM_CKV = MLA_Q_RANK
SM_KPE = SM_CKV + MLA_KV_RANK
SM_BA = SM_KPE + LANES
SM_WIDTH = SM_BA + LANES
MLA_QK = MLA_NOPE + LANES
MLA_TILE = 256
DN_CHUNKS_PER_STEP = 2


def _params(*sem):
    return pltpu.CompilerParams(dimension_semantics=sem, vmem_limit_bytes=VMEM_LIMIT)


def _dot(a, b):
    return jnp.dot(a, b, preferred_element_type=F32)


def _dot_nt(a, b):
    return lax.dot_general(a, b, (((1,), (1,)), ((), ())), preferred_element_type=F32)


def _bdot(a, b):
    return jnp.einsum("hmk,hkn->hmn", a, b, preferred_element_type=F32)


def _bdot_nt(a, b):
    return jnp.einsum("hmk,hnk->hmn", a, b, preferred_element_type=F32)


def _split_bf16(a):
    hi = a.astype(BF16)
    return hi, (a - hi.astype(F32)).astype(BF16)


def _rms(x, gain):
    return x * lax.rsqrt(jnp.mean(x * x, axis=-1, keepdims=True) + EPS) * gain


def _silu(x):
    return x * jax.nn.sigmoid(x)


def _rmsnorm_kernel(x_ref, g_ref, o_ref):
    o_ref[...] = _rms(x_ref[...].astype(F32), g_ref[...]).astype(o_ref.dtype)


def rmsnorm_rows(x, gain, out_dtype, tm=512):
    m, d = x.shape
    return pl.pallas_call(
        _rmsnorm_kernel,
        grid=(m // tm,),
        in_specs=[pl.BlockSpec((tm, d), lambda i: (i, 0)),
                  pl.BlockSpec((1, d), lambda i: (0, 0))],
        out_specs=pl.BlockSpec((tm, d), lambda i: (i, 0)),
        out_shape=jax.ShapeDtypeStruct((m, d), out_dtype),
        compiler_params=_params("parallel"),
        name="rmsnorm_rows",
    )(x, gain.reshape(1, d))


def _weight_spec(w, layer, block, index_fn, **kw):
    if layer is None:
        return pl.BlockSpec(block, index_fn, **kw)
    return pl.BlockSpec((None,) + block, lambda *g: (layer,) + index_fn(*g), **kw)


def _mm_kernel(a_ref, w_ref, o_ref, *scratch):
    if scratch:
        w16_ref, = scratch

        @pl.when(pl.program_id(1) == 0)
        def _():
            w16_ref[...] = w_ref[...].astype(BF16)

        w = w16_ref[...]
    else:
        w = w_ref[...]
    o_ref[...] = _dot(a_ref[...], w).astype(o_ref.dtype)


def matmul(a, w, out_dtype, tm, tn, name, layer=None, n_cols=None):
    m, k = a.shape
    n = w.shape[-1] if n_cols is None else n_cols
    cast = w.dtype != BF16
    return pl.pallas_call(
        _mm_kernel,
        grid=(n // tn, m // tm),
        in_specs=[pl.BlockSpec((tm, k), lambda j, i: (i, 0)),
                  _weight_spec(w, layer, (k, tn), lambda j, i: (0, j))],
        out_specs=pl.BlockSpec((tm, tn), lambda j, i: (i, j)),
        out_shape=jax.ShapeDtypeStruct((m, n), out_dtype),
        scratch_shapes=[pltpu.VMEM((k, tn), BF16)] if cast else [],
        compiler_params=_params("parallel", "arbitrary"),
        name=name,
    )(a, w)


def _mm_res_norm_kernel(*refs, n_parts, emit_h, cast):
    a_refs = refs[:n_parts]
    w_ref, h_ref, g_ref = refs[n_parts:n_parts + 3]
    outs = refs[n_parts + 3:]
    ho_ref = outs[0] if emit_h else None
    uo_ref = outs[1] if emit_h else outs[0]
    if cast:
        w16_ref = outs[-1]

        @pl.when(pl.program_id(0) == 0)
        def _():
            w16_ref[...] = w_ref[...].astype(BF16)

        w_ref = w16_ref
    off = 0
    acc = None
    for a_ref in a_refs:
        kp = a_ref.shape[1]
        part = _dot(a_ref[...], w_ref[off:off + kp, :])
        acc = part if acc is None else acc + part
        off += kp
    h = h_ref[...] + acc
    if emit_h:
        ho_ref[...] = h
    uo_ref[...] = _rms(h, g_ref[...]).astype(uo_ref.dtype)


def matmul_residual_norm(a_parts, w, h, gain, u_dtype, tm, name, layer=None, emit_h=True):
    m, n = h.shape
    k = w.shape[-2]
    n_parts = len(a_parts)
    cast = w.dtype != BF16
    row = lambda i: (i, 0)
    in_specs = [pl.BlockSpec((tm, a.shape[1]), row) for a in a_parts] + [
        _weight_spec(w, layer, (k, n), lambda i: (0, 0), pipeline_mode=pl.Buffered(1)),
        pl.BlockSpec((tm, n), row),
        pl.BlockSpec((1, n), lambda i: (0, 0)),
    ]
    out_specs = [pl.BlockSpec((tm, n), row)]
    out_shape = [jax.ShapeDtypeStruct((m, n), u_dtype)]
    if emit_h:
        out_specs = [pl.BlockSpec((tm, n), row)] + out_specs
        out_shape = [jax.ShapeDtypeStruct((m, n), F32)] + out_shape
    res = pl.pallas_call(
        functools.partial(_mm_res_norm_kernel, n_parts=n_parts, emit_h=emit_h, cast=cast),
        grid=(m // tm,),
        in_specs=in_specs,
        out_specs=out_specs,
        out_shape=out_shape,
        scratch_shapes=[pltpu.VMEM((k, n), BF16)] if cast else [],
        compiler_params=_params("arbitrary"),
        name=name,
    )(*a_parts, w, h, gain.reshape(1, n))
    return tuple(res) if emit_h else res[0]


def _rope_table_kernel(pos_ref, inv_ref, c_ref, sa_ref, sb_ref):
    ang = pos_ref[...].astype(F32) * inv_ref[...]
    c = jnp.cos(ang)
    s = jnp.sin(ang)
    lane = lax.broadcasted_iota(jnp.int32, ang.shape, 1)
    half = MLA_ROPE // 2
    c_ref[...] = jnp.where(lane < MLA_ROPE, c, 0.0)
    sa_ref[...] = jnp.where(lane < half, -s, 0.0)
    sb_ref[...] = jnp.where((lane >= half) & (lane < MLA_ROPE), s, 0.0)


def rope_tables(positions, tm=512):
    t = positions.size
    inv = ROPE_BASE ** (-jnp.arange(0, MLA_ROPE, 2, dtype=F32) / MLA_ROPE)
    inv_row = jnp.concatenate([inv, inv, jnp.zeros((LANES - MLA_ROPE,), F32)]).reshape(1, LANES)
    spec = pl.BlockSpec((tm, LANES), lambda i: (i, 0))
    shape = jax.ShapeDtypeStruct((t, LANES), F32)
    return pl.pallas_call(
        _rope_table_kernel,
        grid=(t // tm,),
        in_specs=[pl.BlockSpec((tm, 1), lambda i: (i, 0)),
                  pl.BlockSpec((1, LANES), lambda i: (0, 0))],
        out_specs=[spec, spec, spec],
        out_shape=[shape, shape, shape],
        compiler_params=_params("parallel"),
        name="rope_tables",
    )(positions.reshape(t, 1), inv_row)


def _rope(x, c, sa, sb):
    half = MLA_ROPE // 2
    return x * c + pltpu.roll(x, LANES - half, 1) * sa + pltpu.roll(x, half, 1) * sb


def _deltanet_kernel(x_ref, ba_ref, cw_ref, alog_ref, dtb_ref, onorm_ref, o_ref, xs_ref, state_ref, *, nch):
    C, D, H = CHUNK, DN_HEAD_DIM, DN_HEADS
    W3 = 3 * DN_WIDTH
    TS = nch * C
    pairs = [(c, h) for c in range(nch) for h in range(H)]

    @pl.when(pl.program_id(1) == 0)
    def _():
        xs_ref[0:SUBLANES, :] = jnp.zeros((SUBLANES, W3), F32)
        state_ref[...] = jnp.zeros_like(state_ref)

    x = x_ref[:, 0:W3]
    xs_ref[SUBLANES:SUBLANES + TS, :] = x
    cw = cw_ref[...]
    y = cw[DN_CONV - 1:DN_CONV, :] * x
    for j in range(DN_CONV - 1):
        shift = DN_CONV - 1 - j
        y = y + cw[j:j + 1, :] * xs_ref[SUBLANES - shift:SUBLANES - shift + TS, :]
    xs_ref[0:SUBLANES, :] = x[TS - SUBLANES:TS, :]
    y = _silu(y)

    ba = ba_ref[...]
    beta_all = jax.nn.sigmoid(ba)
    sp_in = ba + dtb_ref[...]
    softplus = jnp.maximum(sp_in, 0.0) + jnp.log1p(jnp.exp(-jnp.abs(sp_in)))
    g_all = -jnp.exp(alog_ref[...]) * softplus
    row_in_chunk = lax.broadcasted_iota(jnp.int32, (TS, LANES), 0) & (C - 1)
    G = g_all
    step = 1
    while step < C:
        G = G + jnp.where(row_in_chunk >= step, pltpu.roll(G, step, 0), 0.0)
        step *= 2
    Gc = [G[c * C:(c + 1) * C, :] for c in range(nch)]
    GTc = [g.T for g in Gc]
    g_last = [g[C - 1:C, :] for g in Gc]
    e_g = jnp.exp(G)
    e_rem = [jnp.exp(g_last[c] - Gc[c]) for c in range(nch)]
    e_last = [jnp.exp(g) for g in g_last]

    ri = lax.broadcasted_iota(jnp.int32, (C, C), 0)
    ci = lax.broadcasted_iota(jnp.int32, (C, C), 1)
    incl = (ri >= ci)[None]
    strict = (ri > ci)[None]
    eye = jnp.where(ri == ci, 1.0, 0.0).astype(F32)[None]
    onorm = onorm_ref[...]

    def heads(off):
        return jnp.stack([y[c * C:(c + 1) * C, off + h * D:off + (h + 1) * D] for c, h in pairs])

    def head_cols(a, lane0):
        return jnp.stack([a[c * C:(c + 1) * C, lane0 + h:lane0 + h + 1] for c, h in pairs])

    q = heads(0)
    k = heads(DN_WIDTH)
    v = heads(2 * DN_WIDTH)
    q = q * lax.rsqrt(jnp.sum(q * q, axis=-1, keepdims=True) + EPS) * (D ** -0.5)
    k = k * lax.rsqrt(jnp.sum(k * k, axis=-1, keepdims=True) + EPS)
    beta = head_cols(beta_all, 0)
    g_col = head_cols(G, H)
    g_row = jnp.stack([GTc[c][H + h:H + h + 1, :] for c, h in pairs])
    eg_col = head_cols(e_g, H)
    erem_col = jnp.stack([e_rem[c][:, H + h:H + h + 1] for c, h in pairs])
    elast = jnp.stack([e_last[c][:, H + h:H + h + 1] for c, h in pairs])
    decay = jnp.exp(jnp.where(incl, g_col - g_row, -jnp.inf))
    kb = k * beta
    k16 = k.astype(BF16)
    kq = _bdot_nt(jnp.concatenate([kb.astype(BF16), q.astype(BF16)], axis=1), k16)
    lower = jnp.where(strict, kq[:, 0:C] * decay, 0.0)
    attn16 = (kq[:, C:2 * C] * decay).astype(BF16)

    lh, ll = _split_bf16(lower)
    r = _bdot(jnp.concatenate([lh, ll], axis=1), lh)
    p = r[:, 0:C] + r[:, C:2 * C] + _bdot(lh, ll)
    t_inv = eye - lower
    sq = 2
    while True:
        th, tl = _split_bf16(t_inv)
        ph, pl_ = _split_bf16(p)
        if sq * 2 >= C:
            r1 = _bdot(jnp.concatenate([th, tl], axis=1), ph)
            t_inv = t_inv + (r1[:, 0:C] + r1[:, C:2 * C] + _bdot(th, pl_))
            break
        r1 = _bdot(jnp.concatenate([th, tl, ph, pl_], axis=1), ph)
        r2 = _bdot(jnp.concatenate([th, ph], axis=1), pl_)
        t_inv = t_inv + (r1[:, 0:C] + r1[:, C:2 * C] + r2[:, 0:C])
        p = r1[:, 2 * C:3 * C] + r1[:, 3 * C:4 * C] + r2[:, C:2 * C]
        sq *= 2

    rhs = jnp.concatenate([v * beta, kb * eg_col], axis=2)
    sol = _bdot(t_inv.astype(BF16), rhs.astype(BF16))
    u = sol[:, :, 0:D]
    wq16 = jnp.concatenate([sol[:, :, D:2 * D].astype(BF16), (q * eg_col).astype(BF16)], axis=1)
    kdec16 = (k * erem_col).astype(BF16)

    s = state_ref[...]
    for c in range(nch):
        sl = slice(c * H, (c + 1) * H)
        ws = _bdot(wq16[sl], s.astype(BF16))
        v_new = u[sl] - ws[:, 0:C]
        vn16 = v_new.astype(BF16)
        o = ws[:, C:2 * C] + _bdot(attn16[sl], vn16)
        s = s * elast[sl] + jnp.einsum("hck,hcv->hkv", kdec16[sl], vn16, preferred_element_type=F32)
        on = _rms(o, onorm[None])
        for h in range(H):
            z = x_ref[c * C:(c + 1) * C, W3 + h * D:W3 + (h + 1) * D]
            o_ref[c * C:(c + 1) * C, h * D:(h + 1) * D] = (on[h] * _silu(z)).astype(o_ref.dtype)
    state_ref[...] = s


def deltanet(proj_main, proj_small, conv_w, a_log, dt_bias, out_norm, batch, seq, nch=DN_CHUNKS_PER_STEP):
    t = batch * seq
    ts = nch * CHUNK
    ns = seq // ts
    pad = jnp.zeros((LANES - 2 * DN_HEADS,), F32)
    alog_row = jnp.concatenate([jnp.zeros((DN_HEADS,), F32), a_log, pad]).reshape(1, LANES)
    dtb_row = jnp.concatenate([jnp.zeros((DN_HEADS,), F32), dt_bias, pad]).reshape(1, LANES)
    w4 = 4 * DN_WIDTH
    return pl.pallas_call(
        functools.partial(_deltanet_kernel, nch=nch),
        grid=(batch, ns),
        in_specs=[pl.BlockSpec((ts, w4), lambda b, c: (b * ns + c, 0)),
                  pl.BlockSpec((ts, LANES), lambda b, c: (b * ns + c, SM_BA // LANES)),
                  pl.BlockSpec((DN_CONV, 3 * DN_WIDTH), lambda b, c: (0, 0)),
                  pl.BlockSpec((1, LANES), lambda b, c: (0, 0)),
                  pl.BlockSpec((1, LANES), lambda b, c: (0, 0)),
                  pl.BlockSpec((1, DN_HEAD_DIM), lambda b, c: (0, 0))],
        out_specs=pl.BlockSpec((ts, DN_WIDTH), lambda b, c: (b * ns + c, 0)),
        out_shape=jax.ShapeDtypeStruct((t, DN_WIDTH), BF16),
        scratch_shapes=[pltpu.VMEM((SUBLANES + ts, 3 * DN_WIDTH), F32),
                        pltpu.VMEM((DN_HEADS, DN_HEAD_DIM, DN_HEAD_DIM), F32)],
        compiler_params=_params("arbitrary", "arbitrary"),
        name="deltanet",
    )(proj_main, proj_small, conv_w, alog_row, dtb_row, out_norm.reshape(1, DN_HEAD_DIM))


def _mla_prep_kernel(x_ref, c_ref, sa_ref, sb_ref, qn_ref, kvn_ref, wq_ref, wkv_ref, qt_ref, k_ref, vt_ref):
    c, sa, sb = c_ref[...], sa_ref[...], sb_ref[...]
    scale = (MLA_NOPE + MLA_ROPE) ** -0.5
    q_lat = _rms(x_ref[:, SM_Q:SM_Q + MLA_Q_RANK], qn_ref[...])
    qf = _dot(q_lat.astype(BF16), wq_ref[...]) * scale
    c_kv = _rms(x_ref[:, SM_CKV:SM_CKV + MLA_KV_RANK], kvn_ref[...])
    kvf = _dot(c_kv.astype(BF16), wkv_ref[...])
    k_pe = _rope(x_ref[:, SM_KPE:SM_KPE + LANES], c, sa, sb).astype(k_ref.dtype)
    for h in range(MLA_HEADS):
        o = h * MLA_QK
        q_h = jnp.concatenate([qf[:, o:o + MLA_NOPE], _rope(qf[:, o + MLA_NOPE:o + MLA_QK], c, sa, sb)], axis=1)
        qt_ref[o:o + MLA_QK, :] = q_h.T.astype(qt_ref.dtype)
        k_ref[:, o:o + MLA_NOPE] = kvf[:, o:o + MLA_NOPE].astype(k_ref.dtype)
        k_ref[:, o + MLA_NOPE:o + MLA_QK] = k_pe
        vt_ref[h * MLA_V:(h + 1) * MLA_V, :] = kvf[:, o + MLA_NOPE:o + MLA_QK].T.astype(vt_ref.dtype)


def mla_prep(proj_small, rope_c, rope_sa, rope_sb, q_norm, kv_norm, wq, wkv, batch, seq, tm):
    t = proj_small.shape[0]
    ns = seq // tm
    hq = MLA_HEADS * MLA_QK
    hv = MLA_HEADS * MLA_V
    row = lambda i: (i, 0)
    fixed = lambda i: (0, 0)
    return pl.pallas_call(
        _mla_prep_kernel,
        grid=(t // tm,),
        in_specs=[pl.BlockSpec((tm, SM_WIDTH), row),
                  pl.BlockSpec((tm, LANES), row), pl.BlockSpec((tm, LANES), row), pl.BlockSpec((tm, LANES), row),
                  pl.BlockSpec((1, MLA_Q_RANK), fixed), pl.BlockSpec((1, MLA_KV_RANK), fixed),
                  pl.BlockSpec((MLA_Q_RANK, hq), fixed), pl.BlockSpec((MLA_KV_RANK, hq), fixed)],
        out_specs=[pl.BlockSpec((None, hq, tm), lambda i: (i // ns, 0, i % ns)),
                   pl.BlockSpec((tm, hq), row),
                   pl.BlockSpec((None, None, hv, tm), lambda i: (i // ns, i % ns, 0, 0))],
        out_shape=[jax.ShapeDtypeStruct((batch, hq, seq), BF16),
                   jax.ShapeDtypeStruct((t, hq), BF16),
                   jax.ShapeDtypeStruct((batch, ns, hv, tm), BF16)],
        compiler_params=_params("parallel"),
        name="mla_prep",
    )(proj_small, rope_c, rope_sa, rope_sb, q_norm.reshape(1, -1), kv_norm.reshape(1, -1), wq, wkv)


def _mla_attn_kernel(qt_ref, k_ref, vt_ref, o_ref, *, tq):
    qi = pl.program_id(1)
    H = MLA_HEADS
    qt = qt_ref[...].reshape(H, MLA_QK, tq)

    def scores(j):
        start = pl.multiple_of(j * tq, tq)
        kk = k_ref[pl.ds(start, tq), :]
        kh = jnp.stack([kk[:, h * MLA_QK:(h + 1) * MLA_QK] for h in range(H)])
        return _bdot(kh, qt)

    def update(carry, s, vtj):
        m, l, acc = carry
        m_new = jnp.maximum(m, jnp.max(s, axis=1, keepdims=True))
        alpha = jnp.exp(m - m_new)
        p = jnp.exp(s - m_new)
        l = alpha * l + jnp.sum(p, axis=1, keepdims=True)
        acc = alpha * acc + _bdot(vtj.reshape(H, MLA_V, tq), p.astype(BF16))
        return m_new, l, acc

    def body(j, carry):
        return update(carry, scores(j), vt_ref[j])

    init = (jnp.full((H, 1, tq), -jnp.inf, F32), jnp.zeros((H, 1, tq), F32), jnp.zeros((H, MLA_V, tq), F32))
    carry = lax.fori_loop(0, qi, body, init)
    chunk_bits = CHUNK.bit_length() - 1
    kc = lax.shift_right_logical(lax.broadcasted_iota(jnp.int32, (tq, tq), 0), chunk_bits)
    qc = lax.shift_right_logical(lax.broadcasted_iota(jnp.int32, (tq, tq), 1), chunk_bits)
    s = jnp.where((kc <= qc)[None], scores(qi), -jnp.inf)
    _, l, acc = update(carry, s, vt_ref[qi])
    o = acc / l
    for h in range(H):
        o_ref[:, h * MLA_V:(h + 1) * MLA_V] = o[h].T.astype(o_ref.dtype)


def mla_attention(qt, k, vt, batch, seq, tq):
    t = batch * seq
    nq = seq // tq
    hq = MLA_HEADS * MLA_QK
    hv = MLA_HEADS * MLA_V
    return pl.pallas_call(
        functools.partial(_mla_attn_kernel, tq=tq),
        grid=(batch, nq),
        in_specs=[pl.BlockSpec((None, hq, tq), lambda b, i: (b, 0, i)),
                  pl.BlockSpec((seq, hq), lambda b, i: (b, 0)),
                  pl.BlockSpec((None, nq, hv, tq), lambda b, i: (b, 0, 0, 0))],
        out_specs=pl.BlockSpec((tq, hv), lambda b, i: (b * nq + i, 0)),
        out_shape=jax.ShapeDtypeStruct((t, hv), BF16),
        compiler_params=_params("parallel", "parallel"),
        name="mla_attention",
    )(qt, k, vt)


def _xattn_kernel(q_ref, k_ref, v_ref, o_ref):
    scale = XA_HEAD_DIM ** -0.5
    for h in range(XA_HEADS):
        sl = slice(h * XA_HEAD_DIM, (h + 1) * XA_HEAD_DIM)
        s = _dot_nt(q_ref[:, sl], k_ref[:, sl]) * scale
        p = jnp.exp(s - jnp.max(s, axis=-1, keepdims=True))
        l = jnp.sum(p, axis=-1, keepdims=True)
        o_ref[:, sl] = (_dot(p.astype(BF16), v_ref[:, sl]) / l).astype(o_ref.dtype)


def cross_attention(q, k, v, batch, seq, n_mem, tq=512):
    t, d = q.shape
    nq = seq // tq
    return pl.pallas_call(
        _xattn_kernel,
        grid=(batch, nq),
        in_specs=[pl.BlockSpec((tq, d), lambda b, i: (b * nq + i, 0)),
                  pl.BlockSpec((n_mem, d), lambda b, i: (b, 0)),
                  pl.BlockSpec((n_mem, d), lambda b, i: (b, 0))],
        out_specs=pl.BlockSpec((tq, d), lambda b, i: (b * nq + i, 0)),
        out_shape=jax.ShapeDtypeStruct((t, d), BF16),
        compiler_params=_params("parallel", "parallel"),
        name="cross_attention",
    )(q, k, v)


def _ffn_up_kernel(a_ref, wg_ref, wu_ref, cg_ref, cu_ref, bg_ref, bu_ref, o_ref, eg_ref, eu_ref, wg16_ref, wu16_ref,
                   *, tiles_per_seq):
    i = pl.program_id(1)
    tm = a_ref.shape[0]

    @pl.when(i == 0)
    def _():
        wg16_ref[...] = wg_ref[...].astype(BF16)
        wu16_ref[...] = wu_ref[...].astype(BF16)

    @pl.when(i % tiles_per_seq == 0)
    def _():
        eg_ref[0:SUBLANES, :] = jnp.zeros((SUBLANES, eg_ref.shape[1]), F32)
        eu_ref[0:SUBLANES, :] = jnp.zeros((SUBLANES, eu_ref.shape[1]), F32)

    a = a_ref[...]

    def conv(w_ref, cw_ref, b_ref, e_ref):
        acc = _dot(a, w_ref[...])
        e_ref[SUBLANES:SUBLANES + tm, :] = acc
        cw = cw_ref[...]
        y = cw[FFN_CONV - 1:FFN_CONV, :] * acc + b_ref[...]
        for j in range(FFN_CONV - 1):
            shift = FFN_CONV - 1 - j
            y = y + cw[j:j + 1, :] * e_ref[SUBLANES - shift:SUBLANES - shift + tm, :]
        e_ref[0:SUBLANES, :] = acc[tm - SUBLANES:tm, :]
        return y

    gate = conv(wg16_ref, cg_ref, bg_ref, eg_ref)
    up = conv(wu16_ref, cu_ref, bu_ref, eu_ref)
    o_ref[...] = (_silu(gate) * up).astype(o_ref.dtype)


def ffn_up(a, w_up, layer, conv_w, conv_b, seq, tm=1024, tn=512):
    m, k = a.shape
    nb = D_FF // tn
    return pl.pallas_call(
        functools.partial(_ffn_up_kernel, tiles_per_seq=seq // tm),
        grid=(nb, m // tm),
        in_specs=[pl.BlockSpec((tm, k), lambda j, i: (i, 0)),
                  _weight_spec(w_up, layer, (k, tn), lambda j, i: (0, j)),
                  _weight_spec(w_up, layer, (k, tn), lambda j, i: (0, j + nb)),
                  pl.BlockSpec((FFN_CONV, tn), lambda j, i: (0, j)),
                  pl.BlockSpec((FFN_CONV, tn), lambda j, i: (0, j + nb)),
                  pl.BlockSpec((1, tn), lambda j, i: (0, j)),
                  pl.BlockSpec((1, tn), lambda j, i: (0, j + nb))],
        out_specs=pl.BlockSpec((tm, tn), lambda j, i: (i, j)),
        out_shape=jax.ShapeDtypeStruct((m, D_FF), BF16),
        scratch_shapes=[pltpu.VMEM((SUBLANES + tm, tn), F32), pltpu.VMEM((SUBLANES + tm, tn), F32),
                        pltpu.VMEM((k, tn), BF16), pltpu.VMEM((k, tn), BF16)],
        compiler_params=_params("parallel", "arbitrary"),
        name="ffn_up",
    )(a, w_up, w_up, conv_w, conv_w, conv_b.reshape(1, -1), conv_b.reshape(1, -1))


def _prep_in_small(w_in_l):
    d = w_in_l.shape[0]
    base = 4 * DN_WIDTH
    wb = w_in_l[:, base:base + DN_HEADS]
    wa = w_in_l[:, base + DN_HEADS:base + 2 * DN_HEADS]
    mq0 = base + 2 * DN_HEADS
    wq = w_in_l[:, mq0:mq0 + MLA_Q_RANK]
    wckv = w_in_l[:, mq0 + MLA_Q_RANK:mq0 + MLA_Q_RANK + MLA_KV_RANK]
    wkpe = w_in_l[:, mq0 + MLA_Q_RANK + MLA_KV_RANK:mq0 + MLA_Q_RANK + MLA_KV_RANK + MLA_ROPE]
    z = lambda n: jnp.zeros((d, n), w_in_l.dtype)
    return jnp.concatenate([wq, wckv, wkpe, z(LANES - MLA_ROPE), wb, wa, z(LANES - 2 * DN_HEADS)], axis=1)


def _prep_wqb(w):
    r = w.shape[0]
    w3 = w.reshape(r, MLA_HEADS, MLA_NOPE + MLA_ROPE)
    w3 = jnp.concatenate([w3, jnp.zeros((r, MLA_HEADS, LANES - MLA_ROPE), w.dtype)], axis=2)
    return w3.reshape(r, MLA_HEADS * MLA_QK)


def kernel(x, mem, positions, norm_mix, w_in, dn_conv, dn_a_log, dn_dt_bias, dn_out_norm, mla_q_norm,
           mla_w_qb, mla_kv_norm, mla_w_kvb, w_out, mem_norm, norm_xattn, xa_wq, xa_wk, xa_wv, xa_wo,
           norm_ffn, ffn_w_up, ffn_conv, ffn_conv_bias, ffn_w_down, norm_final):
    batch, seq, d = x.shape
    n_mem = mem.shape[1]
    t = batch * seq
    depth = w_in.shape[0]

    rope_c, rope_sa, rope_sb = rope_tables(positions)
    mem_n = rmsnorm_rows(mem.reshape(batch * n_mem, d), mem_norm, BF16, tm=256)
    h = x.reshape(t, d)
    u = rmsnorm_rows(h, norm_mix[0], BF16)

    for l in range(depth):
        w_small = _prep_in_small(w_in[l]).astype(BF16)
        proj_main = matmul(u, w_in, F32, tm=1024, tn=1024, name="in_proj_main", layer=l, n_cols=4 * DN_WIDTH)
        proj_small = matmul(u, w_small, F32, tm=1024, tn=SM_WIDTH, name="in_proj_small")

        o_dn = deltanet(proj_main, proj_small, dn_conv[l], dn_a_log[l], dn_dt_bias[l], dn_out_norm[l], batch, seq)

        qt, k, vt = mla_prep(proj_small, rope_c, rope_sa, rope_sb, mla_q_norm[l], mla_kv_norm[l],
                             _prep_wqb(mla_w_qb[l]).astype(BF16), mla_w_kvb[l].astype(BF16),
                             batch, seq, tm=MLA_TILE)
        o_mla = mla_attention(qt, k, vt, batch, seq, tq=MLA_TILE)

        h, u = matmul_residual_norm([o_dn, o_mla], w_out, h, norm_xattn[l], BF16, tm=256, name="out_proj", layer=l)

        xq = matmul(u, xa_wq, BF16, tm=1024, tn=1024, name="xa_q", layer=l)
        xk = matmul(mem_n, xa_wk, BF16, tm=1024, tn=1024, name="xa_k", layer=l)
        xv = matmul(mem_n, xa_wv, BF16, tm=1024, tn=1024, name="xa_v", layer=l)
        o_xa = cross_attention(xq, xk, xv, batch, seq, n_mem)
        h, u = matmul_residual_norm([o_xa], xa_wo, h, norm_ffn[l], BF16, tm=256, name="xa_o", layer=l)

        hid = ffn_up(u, ffn_w_up, l, ffn_conv[l], ffn_conv_bias[l], seq)
        w_down = ffn_w_down[l].astype(BF16)
        if l + 1 < depth:
            h, u = matmul_residual_norm([hid], w_down, h, norm_mix[l + 1], BF16, tm=256, name="ffn_down")
        else:
            out = matmul_residual_norm([hid], w_down, h, norm_final, x.dtype, tm=256, emit_h=False,
                                       name="ffn_down_final")
    return out.reshape(batch, seq, d)
```

```python
import functools

import numpy as np
import jax
import jax.numpy as jnp
from jax import lax
from jax.experimental import pallas as pl
from jax.experimental.pallas import tpu as pltpu

F32 = jnp.float32
BF16 = jnp.bfloat16

D_MODEL = 2048
DEPTH = 4
CHUNK = 64
DN_HEADS = 8
DN_HEAD_DIM = 128
DN_WIDTH = DN_HEADS * DN_HEAD_DIM
DN_CONV = 4
MLA_HEADS = 8
MLA_NOPE = 128
MLA_ROPE = 64
MLA_V = 128
MLA_Q_RANK = 512
MLA_KV_RANK = 256
ROPE_BASE = 10000.0
XA_HEADS = 4
XA_HEAD_DIM = D_MODEL // XA_HEADS
D_FF = 5632
FFN_CONV = 3
EPS = 1e-6
LOG2_E = float(np.log2(np.e))

LANES = 128
SUBLANES = 8
MXU_COLS = 256
VMEM_LIMIT = 56 * 1024 * 1024

SM_Q = 0
SM_CKV = MLA_Q_RANK
SM_KPE = SM_CKV + MLA_KV_RANK
SM_BA = SM_KPE + LANES
SM_WIDTH = SM_BA + LANES
MLA_QK = MLA_NOPE + LANES
MLA_TILE = 256
DN_CHUNKS_PER_STEP = 2


def _params(*sem):
    return pltpu.CompilerParams(dimension_semantics=sem, vmem_limit_bytes=VMEM_LIMIT)


def _dot(a, b):
    return jnp.dot(a, b, preferred_element_type=F32)


def _dot_nt(a, b):
    return lax.dot_general(a, b, (((1,), (1,)), ((), ())), preferred_element_type=F32)


def _bdot(a, b):
    return jnp.einsum("hmk,hkn->hmn", a, b, preferred_element_type=F32)


def _bdot_nt(a, b):
    return jnp.einsum("hmk,hnk->hmn", a, b, preferred_element_type=F32)


def _rms(x, gain):
    return x * lax.rsqrt(jnp.mean(x * x, axis=-1, keepdims=True) + EPS) * gain


def _silu(x):
    return x * jax.nn.sigmoid(x)


def _rmsnorm_kernel(x_ref, g_ref, o_ref):
    o_ref[...] = _rms(x_ref[...].astype(F32), g_ref[...]).astype(o_ref.dtype)


def rmsnorm_rows(x, gain, out_dtype, tm=512):
    m, d = x.shape
    return pl.pallas_call(
        _rmsnorm_kernel,
        grid=(m // tm,),
        in_specs=[pl.BlockSpec((tm, d), lambda i: (i, 0)),
                  pl.BlockSpec((1, d), lambda i: (0, 0))],
        out_specs=pl.BlockSpec((tm, d), lambda i: (i, 0)),
        out_shape=jax.ShapeDtypeStruct((m, d), out_dtype),
        compiler_params=_params("parallel"),
        name="rmsnorm_rows",
    )(x, gain.reshape(1, d))


def _weight_spec(w, layer, block, index_fn, **kw):
    if layer is None:
        return pl.BlockSpec(block, index_fn, **kw)
    return pl.BlockSpec((None,) + block, lambda *g: (layer,) + index_fn(*g), **kw)


def _mm_kernel(a_ref, w_ref, o_ref, *scratch, w_transposed):
    if scratch:
        w16_ref, = scratch

        @pl.when(pl.program_id(1) == 0)
        def _():
            if w_transposed:
                for c in range(0, w_ref.shape[0], MXU_COLS):
                    w16_ref[:, c:c + MXU_COLS] = w_ref[c:c + MXU_COLS, :].T.astype(BF16)
            else:
                w16_ref[...] = w_ref[...].astype(BF16)

        w = w16_ref[...]
    else:
        w = w_ref[...]
    o_ref[...] = _dot(a_ref[...], w).astype(o_ref.dtype)


def matmul(a, w, out_dtype, tm, tn, name, layer=None, n_cols=None, w_transposed=False):
    m, k = a.shape
    cast = w.dtype != BF16
    assert cast or not w_transposed
    n = w.shape[-2 if w_transposed else -1] if n_cols is None else n_cols
    if w_transposed:
        w_spec = _weight_spec(w, layer, (tn, k), lambda j, i: (j, 0))
    else:
        w_spec = _weight_spec(w, layer, (k, tn), lambda j, i: (0, j))
    return pl.pallas_call(
        functools.partial(_mm_kernel, w_transposed=w_transposed),
        grid=(n // tn, m // tm),
        in_specs=[pl.BlockSpec((tm, k), lambda j, i: (i, 0)), w_spec],
        out_specs=pl.BlockSpec((tm, tn), lambda j, i: (i, j)),
        out_shape=jax.ShapeDtypeStruct((m, n), out_dtype),
        scratch_shapes=[pltpu.VMEM((k, tn), BF16)] if cast else [],
        compiler_params=_params("parallel", "arbitrary"),
        name=name,
    )(a, w)


def _mm_res_norm_kernel(*refs, n_parts, emit_h, cast):
    a_refs = refs[:n_parts]
    w_ref, h_ref, g_ref = refs[n_parts:n_parts + 3]
    outs = refs[n_parts + 3:]
    ho_ref = outs[0] if emit_h else None
    uo_ref = outs[1] if emit_h else outs[0]
    if cast:
        w16_ref = outs[-1]

        @pl.when(pl.program_id(0) == 0)
        def _():
            w16_ref[...] = w_ref[...].astype(BF16)

        w_ref = w16_ref
    off = 0
    acc = None
    for a_ref in a_refs:
        kp = a_ref.shape[1]
        part = _dot(a_ref[...], w_ref[off:off + kp, :])
        acc = part if acc is None else acc + part
        off += kp
    h = h_ref[...] + acc
    if emit_h:
        ho_ref[...] = h
    uo_ref[...] = _rms(h, g_ref[...]).astype(uo_ref.dtype)


def matmul_residual_norm(a_parts, w, h, gain, u_dtype, tm, name, layer=None, emit_h=True):
    m, n = h.shape
    k = w.shape[-2]
    n_parts = len(a_parts)
    cast = w.dtype != BF16
    row = lambda i: (i, 0)
    in_specs = [pl.BlockSpec((tm, a.shape[1]), row) for a in a_parts] + [
        _weight_spec(w, layer, (k, n), lambda i: (0, 0), pipeline_mode=pl.Buffered(1)),
        pl.BlockSpec((tm, n), row),
        pl.BlockSpec((1, n), lambda i: (0, 0)),
    ]
    out_specs = [pl.BlockSpec((tm, n), row)]
    out_shape = [jax.ShapeDtypeStruct((m, n), u_dtype)]
    if emit_h:
        out_specs = [pl.BlockSpec((tm, n), row)] + out_specs
        out_shape = [jax.ShapeDtypeStruct((m, n), F32)] + out_shape
    res = pl.pallas_call(
        functools.partial(_mm_res_norm_kernel, n_parts=n_parts, emit_h=emit_h, cast=cast),
        grid=(m // tm,),
        in_specs=in_specs,
        out_specs=out_specs,
        out_shape=out_shape,
        scratch_shapes=[pltpu.VMEM((k, n), BF16)] if cast else [],
        compiler_params=_params("arbitrary"),
        name=name,
    )(*a_parts, w, h, gain.reshape(1, n))
    return tuple(res) if emit_h else res[0]


def _rope_table_kernel(pos_ref, inv_ref, c_ref, sa_ref, sb_ref):
    ang = pos_ref[...].astype(F32) * inv_ref[...]
    c = jnp.cos(ang)
    s = jnp.sin(ang)
    lane = lax.broadcasted_iota(jnp.int32, ang.shape, 1)
    half = MLA_ROPE // 2
    c_ref[...] = jnp.where(lane < MLA_ROPE, c, 0.0)
    sa_ref[...] = jnp.where(lane < half, -s, 0.0)
    sb_ref[...] = jnp.where((lane >= half) & (lane < MLA_ROPE), s, 0.0)


def rope_tables(positions, tm=512):
    t = positions.size
    inv = ROPE_BASE ** (-jnp.arange(0, MLA_ROPE, 2, dtype=F32) / MLA_ROPE)
    inv_row = jnp.concatenate([inv, inv, jnp.zeros((LANES - MLA_ROPE,), F32)]).reshape(1, LANES)
    spec = pl.BlockSpec((tm, LANES), lambda i: (i, 0))
    shape = jax.ShapeDtypeStruct((t, LANES), F32)
    return pl.pallas_call(
        _rope_table_kernel,
        grid=(t // tm,),
        in_specs=[pl.BlockSpec((tm, 1), lambda i: (i, 0)),
                  pl.BlockSpec((1, LANES), lambda i: (0, 0))],
        out_specs=[spec, spec, spec],
        out_shape=[shape, shape, shape],
        compiler_params=_params("parallel"),
        name="rope_tables",
    )(positions.reshape(t, 1), inv_row)


def _rope(x, c, sa, sb):
    half = MLA_ROPE // 2
    return x * c + pltpu.roll(x, LANES - half, 1) * sa + pltpu.roll(x, half, 1) * sb


def _deltanet_kernel(x_ref, ba_ref, cw_ref, alog_ref, dtb_ref, onorm_ref, o_ref, xs_ref, state_ref, *, nch):
    C, D, H = CHUNK, DN_HEAD_DIM, DN_HEADS
    W3 = 3 * DN_WIDTH
    TS = nch * C
    pairs = [(c, h) for c in range(nch) for h in range(H)]

    @pl.when(pl.program_id(1) == 0)
    def _():
        xs_ref[0:SUBLANES, :] = jnp.zeros((SUBLANES, W3), F32)
        state_ref[...] = jnp.zeros_like(state_ref)

    x = x_ref[:, 0:W3]
    xs_ref[SUBLANES:SUBLANES + TS, :] = x
    cw = cw_ref[...]
    y = cw[DN_CONV - 1:DN_CONV, :] * x
    for j in range(DN_CONV - 1):
        shift = DN_CONV - 1 - j
        y = y + cw[j:j + 1, :] * xs_ref[SUBLANES - shift:SUBLANES - shift + TS, :]
    xs_ref[0:SUBLANES, :] = x[TS - SUBLANES:TS, :]
    y = _silu(y)

    ba = ba_ref[...]
    beta_all = jax.nn.sigmoid(ba)
    sp_in = ba + dtb_ref[...]
    softplus = jnp.maximum(sp_in, 0.0) + jnp.log1p(jnp.exp(-jnp.abs(sp_in)))
    g_all = -jnp.exp(alog_ref[...]) * softplus
    row_in_chunk = lax.broadcasted_iota(jnp.int32, (TS, LANES), 0) & (C - 1)
    G = g_all
    step = 1
    while step < C:
        G = G + jnp.where(row_in_chunk >= step, pltpu.roll(G, step, 0), 0.0)
        step *= 2
    Gc = [G[c * C:(c + 1) * C, :] for c in range(nch)]
    GTc = [g.T for g in Gc]
    g_last = [g[C - 1:C, :] for g in Gc]
    e_g = jnp.exp(G)
    e_rem = [jnp.exp(g_last[c] - Gc[c]) for c in range(nch)]
    e_last = [jnp.exp(g) for g in g_last]

    ri = lax.broadcasted_iota(jnp.int32, (C, C), 0)
    ci = lax.broadcasted_iota(jnp.int32, (C, C), 1)
    incl = (ri >= ci)[None]
    strict = (ri > ci)[None]
    eye = jnp.where(ri == ci, 1.0, 0.0).astype(F32)[None]
    onorm = onorm_ref[...]

    def heads(off):
        return jnp.stack([y[c * C:(c + 1) * C, off + h * D:off + (h + 1) * D] for c, h in pairs])

    def head_cols(a, lane0):
        return jnp.stack([a[c * C:(c + 1) * C, lane0 + h:lane0 + h + 1] for c, h in pairs])

    q = heads(0)
    k = heads(DN_WIDTH)
    v = heads(2 * DN_WIDTH)
    q = q * lax.rsqrt(jnp.sum(q * q, axis=-1, keepdims=True) + EPS) * (D ** -0.5)
    k = k * lax.rsqrt(jnp.sum(k * k, axis=-1, keepdims=True) + EPS)
    beta = head_cols(beta_all, 0)
    g_col = head_cols(G, H)
    g_row = jnp.stack([GTc[c][H + h:H + h + 1, :] for c, h in pairs])
    eg_col = head_cols(e_g, H)
    erem_col = jnp.stack([e_rem[c][:, H + h:H + h + 1] for c, h in pairs])
    elast = jnp.stack([e_last[c][:, H + h:H + h + 1] for c, h in pairs])
    decay = jnp.exp(jnp.where(incl, g_col - g_row, -jnp.inf))
    kb = k * beta
    k16 = k.astype(BF16)
    kq = _bdot_nt(jnp.concatenate([kb.astype(BF16), q.astype(BF16)], axis=1), k16)
    lower = jnp.where(strict, kq[:, 0:C] * decay, 0.0)
    attn16 = (kq[:, C:2 * C] * decay).astype(BF16)

    def same_block(bits):
        return (ri >> bits) == (ci >> bits)

    base_bits = 3
    ld = jnp.where((strict[0] & same_block(base_bits))[None], lower, 0.0)
    ld16 = ld.astype(BF16)
    t_inv = eye - ld
    p16 = _bdot(ld16, ld16).astype(BF16)
    r = _bdot(jnp.concatenate([t_inv.astype(BF16), p16], axis=1), p16)
    t_inv = t_inv + r[:, 0:C]
    t_inv = t_inv + _bdot(t_inv.astype(BF16), r[:, C:2 * C].astype(BF16))
    for bits in range(base_bits, C.bit_length() - 1):
        off_diag = (strict[0] & same_block(bits + 1) & jnp.logical_not(same_block(bits)))[None]
        t16 = t_inv.astype(BF16)
        x = _bdot(t16, jnp.where(off_diag, lower, 0.0).astype(BF16))
        t_inv = t_inv - _bdot(x.astype(BF16), t16)

    rhs = jnp.concatenate([v * beta, kb * eg_col], axis=2)
    sol = _bdot(t_inv.astype(BF16), rhs.astype(BF16))
    u = sol[:, :, 0:D]
    wq16 = jnp.concatenate([sol[:, :, D:2 * D].astype(BF16), (q * eg_col).astype(BF16)], axis=1)
    kdec16 = (k * erem_col).astype(BF16)

    s = state_ref[...]
    for c in range(nch):
        sl = slice(c * H, (c + 1) * H)
        ws = _bdot(wq16[sl], s.astype(BF16))
        v_new = u[sl] - ws[:, 0:C]
        vn16 = v_new.astype(BF16)
        o = ws[:, C:2 * C] + _bdot(attn16[sl], vn16)
        s = s * elast[sl] + jnp.einsum("hck,hcv->hkv", kdec16[sl], vn16, preferred_element_type=F32)
        on = _rms(o, onorm[None])
        for h in range(H):
            z = x_ref[c * C:(c + 1) * C, W3 + h * D:W3 + (h + 1) * D]
            o_ref[c * C:(c + 1) * C, h * D:(h + 1) * D] = (on[h] * _silu(z)).astype(o_ref.dtype)
    state_ref[...] = s


def deltanet(proj_main, proj_small, conv_w, a_log, dt_bias, out_norm, batch, seq, nch=DN_CHUNKS_PER_STEP):
    t = batch * seq
    ts = nch * CHUNK
    ns = seq // ts
    pad = jnp.zeros((LANES - 2 * DN_HEADS,), F32)
    alog_row = jnp.concatenate([jnp.zeros((DN_HEADS,), F32), a_log, pad]).reshape(1, LANES)
    dtb_row = jnp.concatenate([jnp.zeros((DN_HEADS,), F32), dt_bias, pad]).reshape(1, LANES)
    w4 = 4 * DN_WIDTH
    return pl.pallas_call(
        functools.partial(_deltanet_kernel, nch=nch),
        grid=(batch, ns),
        in_specs=[pl.BlockSpec((ts, w4), lambda b, c: (b * ns + c, 0)),
                  pl.BlockSpec((ts, LANES), lambda b, c: (b * ns + c, SM_BA // LANES)),
                  pl.BlockSpec((DN_CONV, 3 * DN_WIDTH), lambda b, c: (0, 0)),
                  pl.BlockSpec((1, LANES), lambda b, c: (0, 0)),
                  pl.BlockSpec((1, LANES), lambda b, c: (0, 0)),
                  pl.BlockSpec((1, DN_HEAD_DIM), lambda b, c: (0, 0))],
        out_specs=pl.BlockSpec((ts, DN_WIDTH), lambda b, c: (b * ns + c, 0)),
        out_shape=jax.ShapeDtypeStruct((t, DN_WIDTH), BF16),
        scratch_shapes=[pltpu.VMEM((SUBLANES + ts, 3 * DN_WIDTH), F32),
                        pltpu.VMEM((DN_HEADS, DN_HEAD_DIM, DN_HEAD_DIM), F32)],
        compiler_params=_params("arbitrary", "arbitrary"),
        name="deltanet",
    )(proj_main, proj_small, conv_w, alog_row, dtb_row, out_norm.reshape(1, DN_HEAD_DIM))


def _mla_prep_kernel(x_ref, c_ref, sa_ref, sb_ref, qn_ref, kvn_ref, wq_ref, wkv_ref, qt_ref, k_ref, vt_ref):
    c, sa, sb = c_ref[...], sa_ref[...], sb_ref[...]
    scale = (MLA_NOPE + MLA_ROPE) ** -0.5 * LOG2_E
    q_lat = _rms(x_ref[:, SM_Q:SM_Q + MLA_Q_RANK], qn_ref[...])
    qf = _dot(q_lat.astype(BF16), wq_ref[...]) * scale
    c_kv = _rms(x_ref[:, SM_CKV:SM_CKV + MLA_KV_RANK], kvn_ref[...])
    kvf = _dot(c_kv.astype(BF16), wkv_ref[...])
    k_pe = _rope(x_ref[:, SM_KPE:SM_KPE + LANES], c, sa, sb).astype(k_ref.dtype)
    for h in range(MLA_HEADS):
        o = h * MLA_QK
        q_h = jnp.concatenate([qf[:, o:o + MLA_NOPE], _rope(qf[:, o + MLA_NOPE:o + MLA_QK], c, sa, sb)], axis=1)
        qt_ref[o:o + MLA_QK, :] = q_h.T.astype(qt_ref.dtype)
        k_ref[:, o:o + MLA_NOPE] = kvf[:, o:o + MLA_NOPE].astype(k_ref.dtype)
        k_ref[:, o + MLA_NOPE:o + MLA_QK] = k_pe
        vt_ref[h * MLA_V:(h + 1) * MLA_V, :] = kvf[:, o + MLA_NOPE:o + MLA_QK].T.astype(vt_ref.dtype)


def mla_prep(proj_small, rope_c, rope_sa, rope_sb, q_norm, kv_norm, wq, wkv, batch, seq, tm):
    t = proj_small.shape[0]
    ns = seq // tm
    hq = MLA_HEADS * MLA_QK
    hv = MLA_HEADS * MLA_V
    row = lambda i: (i, 0)
    fixed = lambda i: (0, 0)
    return pl.pallas_call(
        _mla_prep_kernel,
        grid=(t // tm,),
        in_specs=[pl.BlockSpec((tm, SM_WIDTH), row),
                  pl.BlockSpec((tm, LANES), row), pl.BlockSpec((tm, LANES), row), pl.BlockSpec((tm, LANES), row),
                  pl.BlockSpec((1, MLA_Q_RANK), fixed), pl.BlockSpec((1, MLA_KV_RANK), fixed),
                  pl.BlockSpec((MLA_Q_RANK, hq), fixed), pl.BlockSpec((MLA_KV_RANK, hq), fixed)],
        out_specs=[pl.BlockSpec((None, hq, tm), lambda i: (i // ns, 0, i % ns)),
                   pl.BlockSpec((tm, hq), row),
                   pl.BlockSpec((None, None, hv, tm), lambda i: (i // ns, i % ns, 0, 0))],
        out_shape=[jax.ShapeDtypeStruct((batch, hq, seq), BF16),
                   jax.ShapeDtypeStruct((t, hq), BF16),
                   jax.ShapeDtypeStruct((batch, ns, hv, tm), BF16)],
        compiler_params=_params("parallel"),
        name="mla_prep",
    )(proj_small, rope_c, rope_sa, rope_sb, q_norm.reshape(1, -1), kv_norm.reshape(1, -1), wq, wkv)


def _mla_attn_kernel(qt_ref, k_ref, vt_ref, o_ref, *, tq):
    qi = pl.program_id(1)
    H = MLA_HEADS
    qt = qt_ref[...].reshape(H, MLA_QK, tq)

    def scores(j):
        start = pl.multiple_of(j * tq, tq)
        kk = k_ref[pl.ds(start, tq), :]
        kh = jnp.stack([kk[:, h * MLA_QK:(h + 1) * MLA_QK] for h in range(H)])
        return _bdot(kh, qt)

    def update(carry, s, vtj):
        m, l, acc = carry
        m_new = jnp.maximum(m, jnp.max(s, axis=1, keepdims=True))
        alpha = jnp.exp2(m - m_new)
        p = jnp.exp2(s - m_new)
        l = alpha * l + jnp.sum(p, axis=1, keepdims=True)
        acc = alpha * acc + _bdot(vtj.reshape(H, MLA_V, tq), p.astype(BF16))
        return m_new, l, acc

    def body(j, carry):
        return update(carry, scores(j), vt_ref[j])

    init = (jnp.full((H, 1, tq), -jnp.inf, F32), jnp.zeros((H, 1, tq), F32), jnp.zeros((H, MLA_V, tq), F32))
    carry = lax.fori_loop(0, qi, body, init)
    chunk_bits = CHUNK.bit_length() - 1
    kc = lax.shift_right_logical(lax.broadcasted_iota(jnp.int32, (tq, tq), 0), chunk_bits)
    qc = lax.shift_right_logical(lax.broadcasted_iota(jnp.int32, (tq, tq), 1), chunk_bits)
    s = jnp.where((kc <= qc)[None], scores(qi), -jnp.inf)
    _, l, acc = update(carry, s, vt_ref[qi])
    o = acc / l
    for h in range(H):
        o_ref[:, h * MLA_V:(h + 1) * MLA_V] = o[h].T.astype(o_ref.dtype)


def mla_attention(qt, k, vt, batch, seq, tq):
    t = batch * seq
    nq = seq // tq
    hq = MLA_HEADS * MLA_QK
    hv = MLA_HEADS * MLA_V
    return pl.pallas_call(
        functools.partial(_mla_attn_kernel, tq=tq),
        grid=(batch, nq),
        in_specs=[pl.BlockSpec((None, hq, tq), lambda b, i: (b, 0, i)),
                  pl.BlockSpec((seq, hq), lambda b, i: (b, 0)),
                  pl.BlockSpec((None, nq, hv, tq), lambda b, i: (b, 0, 0, 0))],
        out_specs=pl.BlockSpec((tq, hv), lambda b, i: (b * nq + i, 0)),
        out_shape=jax.ShapeDtypeStruct((t, hv), BF16),
        compiler_params=_params("parallel", "parallel"),
        name="mla_attention",
    )(qt, k, vt)


def _xattn_kernel(q_ref, k_ref, v_ref, o_ref):
    scale = XA_HEAD_DIM ** -0.5
    for h in range(XA_HEADS):
        sl = slice(h * XA_HEAD_DIM, (h + 1) * XA_HEAD_DIM)
        s = _dot_nt(q_ref[:, sl], k_ref[:, sl]) * scale
        p = jnp.exp(s - jnp.max(s, axis=-1, keepdims=True))
        l = jnp.sum(p, axis=-1, keepdims=True)
        o_ref[:, sl] = (_dot(p.astype(BF16), v_ref[:, sl]) / l).astype(o_ref.dtype)


def cross_attention(q, k, v, batch, seq, n_mem, tq=512):
    t, d = q.shape
    nq = seq // tq
    return pl.pallas_call(
        _xattn_kernel,
        grid=(batch, nq),
        in_specs=[pl.BlockSpec((tq, d), lambda b, i: (b * nq + i, 0)),
                  pl.BlockSpec((n_mem, d), lambda b, i: (b, 0)),
                  pl.BlockSpec((n_mem, d), lambda b, i: (b, 0))],
        out_specs=pl.BlockSpec((tq, d), lambda b, i: (b * nq + i, 0)),
        out_shape=jax.ShapeDtypeStruct((t, d), BF16),
        compiler_params=_params("parallel", "parallel"),
        name="cross_attention",
    )(q, k, v)


def _ffn_up_kernel(a_ref, wg_ref, wu_ref, cg_ref, cu_ref, bg_ref, bu_ref, o_ref, eg_ref, eu_ref, wg16_ref, wu16_ref,
                   *, tiles_per_seq):
    i = pl.program_id(1)
    tm = a_ref.shape[0]

    @pl.when(i == 0)
    def _():
        wg16_ref[...] = wg_ref[...].astype(BF16)
        wu16_ref[...] = wu_ref[...].astype(BF16)

    @pl.when(i % tiles_per_seq == 0)
    def _():
        eg_ref[0:SUBLANES, :] = jnp.zeros((SUBLANES, eg_ref.shape[1]), F32)
        eu_ref[0:SUBLANES, :] = jnp.zeros((SUBLANES, eu_ref.shape[1]), F32)

    a = a_ref[...]

    def conv(w_ref, cw_ref, b_ref, e_ref):
        acc = _dot(a, w_ref[...])
        e_ref[SUBLANES:SUBLANES + tm, :] = acc
        cw = cw_ref[...]
        y = cw[FFN_CONV - 1:FFN_CONV, :] * acc + b_ref[...]
        for j in range(FFN_CONV - 1):
            shift = FFN_CONV - 1 - j
            y = y + cw[j:j + 1, :] * e_ref[SUBLANES - shift:SUBLANES - shift + tm, :]
        e_ref[0:SUBLANES, :] = acc[tm - SUBLANES:tm, :]
        return y

    gate = conv(wg16_ref, cg_ref, bg_ref, eg_ref)
    up = conv(wu16_ref, cu_ref, bu_ref, eu_ref)
    o_ref[...] = (_silu(gate) * up).astype(o_ref.dtype)


def ffn_up(a, w_up, layer, conv_w, conv_b, seq, tm=1024, tn=512):
    m, k = a.shape
    nb = D_FF // tn
    return pl.pallas_call(
        functools.partial(_ffn_up_kernel, tiles_per_seq=seq // tm),
        grid=(nb, m // tm),
        in_specs=[pl.BlockSpec((tm, k), lambda j, i: (i, 0)),
                  _weight_spec(w_up, layer, (k, tn), lambda j, i: (0, j)),
                  _weight_spec(w_up, layer, (k, tn), lambda j, i: (0, j + nb)),
                  pl.BlockSpec((FFN_CONV, tn), lambda j, i: (0, j)),
                  pl.BlockSpec((FFN_CONV, tn), lambda j, i: (0, j + nb)),
                  pl.BlockSpec((1, tn), lambda j, i: (0, j)),
                  pl.BlockSpec((1, tn), lambda j, i: (0, j + nb))],
        out_specs=pl.BlockSpec((tm, tn), lambda j, i: (i, j)),
        out_shape=jax.ShapeDtypeStruct((m, D_FF), BF16),
        scratch_shapes=[pltpu.VMEM((SUBLANES + tm, tn), F32), pltpu.VMEM((SUBLANES + tm, tn), F32),
                        pltpu.VMEM((k, tn), BF16), pltpu.VMEM((k, tn), BF16)],
        compiler_params=_params("parallel", "arbitrary"),
        name="ffn_up",
    )(a, w_up, w_up, conv_w, conv_w, conv_b.reshape(1, -1), conv_b.reshape(1, -1))


def _prep_in_small(w_in_t_l):
    d = w_in_t_l.shape[1]
    base = 4 * DN_WIDTH
    wb = w_in_t_l[base:base + DN_HEADS]
    wa = w_in_t_l[base + DN_HEADS:base + 2 * DN_HEADS]
    mq0 = base + 2 * DN_HEADS
    wq = w_in_t_l[mq0:mq0 + MLA_Q_RANK]
    wckv = w_in_t_l[mq0 + MLA_Q_RANK:mq0 + MLA_Q_RANK + MLA_KV_RANK]
    wkpe = w_in_t_l[mq0 + MLA_Q_RANK + MLA_KV_RANK:mq0 + MLA_Q_RANK + MLA_KV_RANK + MLA_ROPE]
    z = lambda n: jnp.zeros((n, d), w_in_t_l.dtype)
    return jnp.concatenate([wq, wckv, wkpe, z(LANES - MLA_ROPE), wb, wa, z(LANES - 2 * DN_HEADS)], axis=0)


def _prep_wqb(w):
    r = w.shape[0]
    w3 = w.reshape(r, MLA_HEADS, MLA_NOPE + MLA_ROPE)
    w3 = jnp.concatenate([w3, jnp.zeros((r, MLA_HEADS, LANES - MLA_ROPE), w.dtype)], axis=2)
    return w3.reshape(r, MLA_HEADS * MLA_QK)


def kernel(x, mem, positions, norm_mix, w_in, dn_conv, dn_a_log, dn_dt_bias, dn_out_norm, mla_q_norm,
           mla_w_qb, mla_kv_norm, mla_w_kvb, w_out, mem_norm, norm_xattn, xa_wq, xa_wk, xa_wv, xa_wo,
           norm_ffn, ffn_w_up, ffn_conv, ffn_conv_bias, ffn_w_down, norm_final):
    batch, seq, d = x.shape
    n_mem = mem.shape[1]
    t = batch * seq
    depth = w_in.shape[0]

    rope_c, rope_sa, rope_sb = rope_tables(positions)
    mem_n = rmsnorm_rows(mem.reshape(batch * n_mem, d), mem_norm, BF16, tm=256)
    h = x.reshape(t, d)
    u = rmsnorm_rows(h, norm_mix[0], BF16)

    w_in_t = jnp.swapaxes(w_in, 1, 2)
    w_down16 = ffn_w_down.astype(BF16)

    for l in range(depth):
        proj_main = matmul(u, w_in_t, F32, tm=1024, tn=1024, name="in_proj_main", layer=l, n_cols=4 * DN_WIDTH,
                           w_transposed=True)
        proj_small = matmul(u, _prep_in_small(w_in_t[l]), F32, tm=1024, tn=SM_WIDTH, name="in_proj_small",
                            w_transposed=True)

        o_dn = deltanet(proj_main, proj_small, dn_conv[l], dn_a_log[l], dn_dt_bias[l], dn_out_norm[l], batch, seq)

        qt, k, vt = mla_prep(proj_small, rope_c, rope_sa, rope_sb, mla_q_norm[l], mla_kv_norm[l],
                             _prep_wqb(mla_w_qb[l]).astype(BF16), mla_w_kvb[l].astype(BF16),
                             batch, seq, tm=MLA_TILE)
        o_mla = mla_attention(qt, k, vt, batch, seq, tq=MLA_TILE)

        h, u = matmul_residual_norm([o_dn, o_mla], w_out, h, norm_xattn[l], BF16, tm=256, name="out_proj", layer=l)

        xq = matmul(u, xa_wq, BF16, tm=1024, tn=1024, name="xa_q", layer=l)
        xk = matmul(mem_n, xa_wk, BF16, tm=1024, tn=1024, name="xa_k", layer=l)
        xv = matmul(mem_n, xa_wv, BF16, tm=1024, tn=1024, name="xa_v", layer=l)
        o_xa = cross_attention(xq, xk, xv, batch, seq, n_mem)
        h, u = matmul_residual_norm([o_xa], xa_wo, h, norm_ffn[l], BF16, tm=256, name="xa_o", layer=l)

        hid = ffn_up(u, ffn_w_up, l, ffn_conv[l], ffn_conv_bias[l], seq)
        if l + 1 < depth:
            h, u = matmul_residual_norm([hid], w_down16, h, norm_mix[l + 1], BF16, tm=256, name="ffn_down", layer=l)
        else:
            out = matmul_residual_norm([hid], w_down16, h, norm_final, x.dtype, tm=256, emit_h=False,
                                       name="ffn_down_final", layer=l)
    return out.reshape(batch, seq, d)
```

```python
import functools

import numpy as np
import jax
import jax.numpy as jnp
from jax import lax
from jax.experimental import pallas as pl
from jax.experimental.pallas import tpu as pltpu

F32 = jnp.float32
BF16 = jnp.bfloat16

D_MODEL = 2048
DEPTH = 4
CHUNK = 64
DN_HEADS = 8
DN_HEAD_DIM = 128
DN_WIDTH = DN_HEADS * DN_HEAD_DIM
DN_CONV = 4
MLA_HEADS = 8
MLA_NOPE = 128
MLA_ROPE = 64
MLA_V = 128
MLA_Q_RANK = 512
MLA_KV_RANK = 256
ROPE_BASE = 10000.0
XA_HEADS = 4
XA_HEAD_DIM = D_MODEL // XA_HEADS
D_FF = 5632
FFN_CONV = 3
EPS = 1e-6
LOG2_E = float(np.log2(np.e))

LANES = 128
SUBLANES = 8
MXU_COLS = 256
VMEM_LIMIT = 56 * 1024 * 1024

SM_Q = 0
SM_CKV = MLA_Q_RANK
SM_KPE = SM_CKV + MLA_KV_RANK
SM_BA = SM_KPE + LANES
SM_WIDTH = SM_BA + LANES
MLA_QK = MLA_NOPE + LANES
MLA_VT_ROWS = MLA_V + 16
MLA_TILE = 256
DN_CHUNKS_PER_STEP = 2


def _params(*sem):
    return pltpu.CompilerParams(dimension_semantics=sem, vmem_limit_bytes=VMEM_LIMIT)


def _dot(a, b):
    return jnp.dot(a, b, preferred_element_type=F32)


def _dot_nt(a, b):
    return lax.dot_general(a, b, (((1,), (1,)), ((), ())), preferred_element_type=F32)


def _bdot(a, b):
    return jnp.einsum("hmk,hkn->hmn", a, b, preferred_element_type=F32)


def _bdot_nt(a, b):
    return jnp.einsum("hmk,hnk->hmn", a, b, preferred_element_type=F32)


def _rms(x, gain):
    return x * lax.rsqrt(jnp.mean(x * x, axis=-1, keepdims=True) + EPS) * gain


def _silu(x):
    return x * jax.nn.sigmoid(x)


def _rmsnorm_kernel(x_ref, g_ref, o_ref):
    o_ref[...] = _rms(x_ref[...].astype(F32), g_ref[...]).astype(o_ref.dtype)


def rmsnorm_rows(x, gain, out_dtype, tm=512):
    m, d = x.shape
    return pl.pallas_call(
        _rmsnorm_kernel,
        grid=(m // tm,),
        in_specs=[pl.BlockSpec((tm, d), lambda i: (i, 0)),
                  pl.BlockSpec((1, d), lambda i: (0, 0))],
        out_specs=pl.BlockSpec((tm, d), lambda i: (i, 0)),
        out_shape=jax.ShapeDtypeStruct((m, d), out_dtype),
        compiler_params=_params("parallel"),
        name="rmsnorm_rows",
    )(x, gain.reshape(1, d))


def _weight_spec(w, layer, block, index_fn, **kw):
    if layer is None:
        return pl.BlockSpec(block, index_fn, **kw)
    return pl.BlockSpec((None,) + block, lambda *g: (layer,) + index_fn(*g), **kw)


def _mm_kernel(a_ref, w_ref, o_ref, *scratch, w_transposed):
    if scratch:
        w16_ref, = scratch

        @pl.when(pl.program_id(1) == 0)
        def _():
            if w_transposed:
                for c in range(0, w_ref.shape[0], MXU_COLS):
                    w16_ref[:, c:c + MXU_COLS] = w_ref[c:c + MXU_COLS, :].T.astype(BF16)
            else:
                w16_ref[...] = w_ref[...].astype(BF16)

        w = w16_ref[...]
    else:
        w = w_ref[...]
    o_ref[...] = _dot(a_ref[...], w).astype(o_ref.dtype)


def matmul(a, w, out_dtype, tm, tn, name, layer=None, n_cols=None, w_transposed=False):
    m, k = a.shape
    cast = w.dtype != BF16
    assert cast or not w_transposed
    n = w.shape[-2 if w_transposed else -1] if n_cols is None else n_cols
    if w_transposed:
        w_spec = _weight_spec(w, layer, (tn, k), lambda j, i: (j, 0))
    else:
        w_spec = _weight_spec(w, layer, (k, tn), lambda j, i: (0, j))
    return pl.pallas_call(
        functools.partial(_mm_kernel, w_transposed=w_transposed),
        grid=(n // tn, m // tm),
        in_specs=[pl.BlockSpec((tm, k), lambda j, i: (i, 0)), w_spec],
        out_specs=pl.BlockSpec((tm, tn), lambda j, i: (i, j)),
        out_shape=jax.ShapeDtypeStruct((m, n), out_dtype),
        scratch_shapes=[pltpu.VMEM((k, tn), BF16)] if cast else [],
        compiler_params=_params("parallel", "arbitrary"),
        name=name,
    )(a, w)


def _mm_res_norm_kernel(*refs, n_parts, emit_h, cast):
    a_refs = refs[:n_parts]
    w_ref, h_ref, g_ref = refs[n_parts:n_parts + 3]
    outs = refs[n_parts + 3:]
    ho_ref = outs[0] if emit_h else None
    uo_ref = outs[1] if emit_h else outs[0]
    if cast:
        w16_ref = outs[-1]

        @pl.when(pl.program_id(0) == 0)
        def _():
            w16_ref[...] = w_ref[...].astype(BF16)

        w_ref = w16_ref
    off = 0
    acc = None
    for a_ref in a_refs:
        kp = a_ref.shape[1]
        part = _dot(a_ref[...], w_ref[off:off + kp, :])
        acc = part if acc is None else acc + part
        off += kp
    h = h_ref[...] + acc
    if emit_h:
        ho_ref[...] = h
    uo_ref[...] = _rms(h, g_ref[...]).astype(uo_ref.dtype)


def matmul_residual_norm(a_parts, w, h, gain, u_dtype, tm, name, layer=None, emit_h=True):
    m, n = h.shape
    k = w.shape[-2]
    n_parts = len(a_parts)
    cast = w.dtype != BF16
    row = lambda i: (i, 0)
    in_specs = [pl.BlockSpec((tm, a.shape[1]), row) for a in a_parts] + [
        _weight_spec(w, layer, (k, n), lambda i: (0, 0), pipeline_mode=pl.Buffered(1)),
        pl.BlockSpec((tm, n), row),
        pl.BlockSpec((1, n), lambda i: (0, 0)),
    ]
    out_specs = [pl.BlockSpec((tm, n), row)]
    out_shape = [jax.ShapeDtypeStruct((m, n), u_dtype)]
    if emit_h:
        out_specs = [pl.BlockSpec((tm, n), row)] + out_specs
        out_shape = [jax.ShapeDtypeStruct((m, n), F32)] + out_shape
    res = pl.pallas_call(
        functools.partial(_mm_res_norm_kernel, n_parts=n_parts, emit_h=emit_h, cast=cast),
        grid=(m // tm,),
        in_specs=in_specs,
        out_specs=out_specs,
        out_shape=out_shape,
        scratch_shapes=[pltpu.VMEM((k, n), BF16)] if cast else [],
        compiler_params=_params("arbitrary"),
        name=name,
    )(*a_parts, w, h, gain.reshape(1, n))
    return tuple(res) if emit_h else res[0]


def _rope_table_kernel(pos_ref, inv_ref, c_ref, sa_ref, sb_ref):
    ang = pos_ref[...].astype(F32) * inv_ref[...]
    c = jnp.cos(ang)
    s = jnp.sin(ang)
    lane = lax.broadcasted_iota(jnp.int32, ang.shape, 1)
    half = MLA_ROPE // 2
    c_ref[...] = jnp.where(lane < MLA_ROPE, c, 0.0)
    sa_ref[...] = jnp.where(lane < half, -s, 0.0)
    sb_ref[...] = jnp.where((lane >= half) & (lane < MLA_ROPE), s, 0.0)


def rope_tables(positions, tm=512):
    t = positions.size
    inv = ROPE_BASE ** (-jnp.arange(0, MLA_ROPE, 2, dtype=F32) / MLA_ROPE)
    inv_row = jnp.concatenate([inv, inv, jnp.zeros((LANES - MLA_ROPE,), F32)]).reshape(1, LANES)
    spec = pl.BlockSpec((tm, LANES), lambda i: (i, 0))
    shape = jax.ShapeDtypeStruct((t, LANES), F32)
    return pl.pallas_call(
        _rope_table_kernel,
        grid=(t // tm,),
        in_specs=[pl.BlockSpec((tm, 1), lambda i: (i, 0)),
                  pl.BlockSpec((1, LANES), lambda i: (0, 0))],
        out_specs=[spec, spec, spec],
        out_shape=[shape, shape, shape],
        compiler_params=_params("parallel"),
        name="rope_tables",
    )(positions.reshape(t, 1), inv_row)


def _rope(x, c, sa, sb):
    half = MLA_ROPE // 2
    return x * c + pltpu.roll(x, LANES - half, 1) * sa + pltpu.roll(x, half, 1) * sb


def _deltanet_kernel(x_ref, ba_ref, cw_ref, alog_ref, dtb_ref, onorm_ref, o_ref, xs_ref, state_ref, *, nch):
    C, D, H = CHUNK, DN_HEAD_DIM, DN_HEADS
    W3 = 3 * DN_WIDTH
    TS = nch * C
    pairs = [(c, h) for c in range(nch) for h in range(H)]

    @pl.when(pl.program_id(1) == 0)
    def _():
        xs_ref[0:SUBLANES, :] = jnp.zeros((SUBLANES, W3), F32)
        state_ref[...] = jnp.zeros_like(state_ref)

    x = x_ref[:, 0:W3]
    xs_ref[SUBLANES:SUBLANES + TS, :] = x
    cw = cw_ref[...]
    y = cw[DN_CONV - 1:DN_CONV, :] * x
    for j in range(DN_CONV - 1):
        shift = DN_CONV - 1 - j
        y = y + cw[j:j + 1, :] * xs_ref[SUBLANES - shift:SUBLANES - shift + TS, :]
    xs_ref[0:SUBLANES, :] = x[TS - SUBLANES:TS, :]
    y = _silu(y)

    ba = ba_ref[...]
    beta_all = jax.nn.sigmoid(ba)
    sp_in = ba + dtb_ref[...]
    softplus = jnp.maximum(sp_in, 0.0) + jnp.log1p(jnp.exp(-jnp.abs(sp_in)))
    g_all = -jnp.exp(alog_ref[...]) * softplus
    row_in_chunk = lax.broadcasted_iota(jnp.int32, (TS, LANES), 0) & (C - 1)
    G = g_all
    step = 1
    while step < C:
        G = G + jnp.where(row_in_chunk >= step, pltpu.roll(G, step, 0), 0.0)
        step *= 2
    Gc = [G[c * C:(c + 1) * C, :] for c in range(nch)]
    GTc = [g.T for g in Gc]
    g_last = [g[C - 1:C, :] for g in Gc]
    e_g = jnp.exp(G)
    e_rem = [jnp.exp(g_last[c] - Gc[c]) for c in range(nch)]
    e_last = [jnp.exp(g) for g in g_last]

    ri = lax.broadcasted_iota(jnp.int32, (C, C), 0)
    ci = lax.broadcasted_iota(jnp.int32, (C, C), 1)
    incl = (ri >= ci)[None]
    strict = (ri > ci)[None]
    eye = jnp.where(ri == ci, 1.0, 0.0).astype(F32)[None]
    onorm = onorm_ref[...]

    def heads(off):
        return jnp.stack([y[c * C:(c + 1) * C, off + h * D:off + (h + 1) * D] for c, h in pairs])

    def head_cols(a, lane0):
        return jnp.stack([a[c * C:(c + 1) * C, lane0 + h:lane0 + h + 1] for c, h in pairs])

    q = heads(0)
    k = heads(DN_WIDTH)
    v = heads(2 * DN_WIDTH)
    q = q * lax.rsqrt(jnp.sum(q * q, axis=-1, keepdims=True) + EPS) * (D ** -0.5)
    k = k * lax.rsqrt(jnp.sum(k * k, axis=-1, keepdims=True) + EPS)
    beta = head_cols(beta_all, 0)
    g_col = head_cols(G, H)
    g_row = jnp.stack([GTc[c][H + h:H + h + 1, :] for c, h in pairs])
    eg_col = head_cols(e_g, H)
    erem_col = jnp.stack([e_rem[c][:, H + h:H + h + 1] for c, h in pairs])
    elast = jnp.stack([e_last[c][:, H + h:H + h + 1] for c, h in pairs])
    decay = jnp.exp(jnp.where(incl, g_col - g_row, -jnp.inf))
    kb = k * beta
    k16 = k.astype(BF16)
    kq = _bdot_nt(jnp.concatenate([kb.astype(BF16), q.astype(BF16)], axis=1), k16)
    lower = jnp.where(strict, kq[:, 0:C] * decay, 0.0)
    attn16 = (kq[:, C:2 * C] * decay).astype(BF16)

    def same_block(bits):
        return (ri >> bits) == (ci >> bits)

    base_bits = 3
    ld = jnp.where((strict[0] & same_block(base_bits))[None], lower, 0.0)
    ld16 = ld.astype(BF16)
    t_inv = eye - ld
    p16 = _bdot(ld16, ld16).astype(BF16)
    r = _bdot(jnp.concatenate([t_inv.astype(BF16), p16], axis=1), p16)
    t_inv = t_inv + r[:, 0:C]
    t_inv = t_inv + _bdot(t_inv.astype(BF16), r[:, C:2 * C].astype(BF16))
    for bits in range(base_bits, C.bit_length() - 1):
        off_diag = (strict[0] & same_block(bits + 1) & jnp.logical_not(same_block(bits)))[None]
        t16 = t_inv.astype(BF16)
        x = _bdot(t16, jnp.where(off_diag, lower, 0.0).astype(BF16))
        t_inv = t_inv - _bdot(x.astype(BF16), t16)

    rhs = jnp.concatenate([v * beta, kb * eg_col], axis=2)
    sol = _bdot(t_inv.astype(BF16), rhs.astype(BF16))
    u = sol[:, :, 0:D]
    wq16 = jnp.concatenate([sol[:, :, D:2 * D].astype(BF16), (q * eg_col).astype(BF16)], axis=1)
    kdec16 = (k * erem_col).astype(BF16)

    s = state_ref[...]
    for c in range(nch):
        sl = slice(c * H, (c + 1) * H)
        ws = _bdot(wq16[sl], s.astype(BF16))
        v_new = u[sl] - ws[:, 0:C]
        vn16 = v_new.astype(BF16)
        o = ws[:, C:2 * C] + _bdot(attn16[sl], vn16)
        s = s * elast[sl] + jnp.einsum("hck,hcv->hkv", kdec16[sl], vn16, preferred_element_type=F32)
        on = _rms(o, onorm[None])
        for h in range(H):
            z = x_ref[c * C:(c + 1) * C, W3 + h * D:W3 + (h + 1) * D]
            o_ref[c * C:(c + 1) * C, h * D:(h + 1) * D] = (on[h] * _silu(z)).astype(o_ref.dtype)
    state_ref[...] = s


def deltanet(proj_main, proj_small, conv_w, a_log, dt_bias, out_norm, batch, seq, nch=DN_CHUNKS_PER_STEP):
    t = batch * seq
    ts = nch * CHUNK
    ns = seq // ts
    pad = jnp.zeros((LANES - 2 * DN_HEADS,), F32)
    alog_row = jnp.concatenate([jnp.zeros((DN_HEADS,), F32), a_log, pad]).reshape(1, LANES)
    dtb_row = jnp.concatenate([jnp.zeros((DN_HEADS,), F32), dt_bias, pad]).reshape(1, LANES)
    w4 = 4 * DN_WIDTH
    return pl.pallas_call(
        functools.partial(_deltanet_kernel, nch=nch),
        grid=(batch, ns),
        in_specs=[pl.BlockSpec((ts, w4), lambda b, c: (b * ns + c, 0)),
                  pl.BlockSpec((ts, LANES), lambda b, c: (b * ns + c, SM_BA // LANES)),
                  pl.BlockSpec((DN_CONV, 3 * DN_WIDTH), lambda b, c: (0, 0)),
                  pl.BlockSpec((1, LANES), lambda b, c: (0, 0)),
                  pl.BlockSpec((1, LANES), lambda b, c: (0, 0)),
                  pl.BlockSpec((1, DN_HEAD_DIM), lambda b, c: (0, 0))],
        out_specs=pl.BlockSpec((ts, DN_WIDTH), lambda b, c: (b * ns + c, 0)),
        out_shape=jax.ShapeDtypeStruct((t, DN_WIDTH), BF16),
        scratch_shapes=[pltpu.VMEM((SUBLANES + ts, 3 * DN_WIDTH), F32),
                        pltpu.VMEM((DN_HEADS, DN_HEAD_DIM, DN_HEAD_DIM), F32)],
        compiler_params=_params("arbitrary", "arbitrary"),
        name="deltanet",
    )(proj_main, proj_small, conv_w, alog_row, dtb_row, out_norm.reshape(1, DN_HEAD_DIM))


def _mla_prep_kernel(x_ref, c_ref, sa_ref, sb_ref, qn_ref, kvn_ref, wq_ref, wkv_ref, qt_ref, k_ref, vt_ref):
    c, sa, sb = c_ref[...], sa_ref[...], sb_ref[...]
    scale = (MLA_NOPE + MLA_ROPE) ** -0.5 * LOG2_E
    q_lat = _rms(x_ref[:, SM_Q:SM_Q + MLA_Q_RANK], qn_ref[...])
    qf = _dot(q_lat.astype(BF16), wq_ref[...]) * scale
    c_kv = _rms(x_ref[:, SM_CKV:SM_CKV + MLA_KV_RANK], kvn_ref[...])
    kvf = _dot(c_kv.astype(BF16), wkv_ref[...])
    k_pe = _rope(x_ref[:, SM_KPE:SM_KPE + LANES], c, sa, sb).astype(k_ref.dtype)
    pad_rows = MLA_VT_ROWS - MLA_V
    ones_row = (lax.broadcasted_iota(jnp.int32, (pad_rows, x_ref.shape[0]), 0) == 0).astype(vt_ref.dtype)
    for h in range(MLA_HEADS):
        o = h * MLA_QK
        q_h = jnp.concatenate([qf[:, o:o + MLA_NOPE], _rope(qf[:, o + MLA_NOPE:o + MLA_QK], c, sa, sb)], axis=1)
        qt_ref[o:o + MLA_QK, :] = q_h.T.astype(qt_ref.dtype)
        k_ref[:, o:o + MLA_NOPE] = kvf[:, o:o + MLA_NOPE].astype(k_ref.dtype)
        k_ref[:, o + MLA_NOPE:o + MLA_QK] = k_pe
        vt_ref[h * MLA_VT_ROWS:h * MLA_VT_ROWS + MLA_V, :] = kvf[:, o + MLA_NOPE:o + MLA_QK].T.astype(vt_ref.dtype)
        vt_ref[h * MLA_VT_ROWS + MLA_V:(h + 1) * MLA_VT_ROWS, :] = ones_row


def mla_prep(proj_small, rope_c, rope_sa, rope_sb, q_norm, kv_norm, wq, wkv, batch, seq, tm):
    t = proj_small.shape[0]
    ns = seq // tm
    hq = MLA_HEADS * MLA_QK
    hv = MLA_HEADS * MLA_VT_ROWS
    row = lambda i: (i, 0)
    fixed = lambda i: (0, 0)
    return pl.pallas_call(
        _mla_prep_kernel,
        grid=(t // tm,),
        in_specs=[pl.BlockSpec((tm, SM_WIDTH), row),
                  pl.BlockSpec((tm, LANES), row), pl.BlockSpec((tm, LANES), row), pl.BlockSpec((tm, LANES), row),
                  pl.BlockSpec((1, MLA_Q_RANK), fixed), pl.BlockSpec((1, MLA_KV_RANK), fixed),
                  pl.BlockSpec((MLA_Q_RANK, hq), fixed), pl.BlockSpec((MLA_KV_RANK, hq), fixed)],
        out_specs=[pl.BlockSpec((None, hq, tm), lambda i: (i // ns, 0, i % ns)),
                   pl.BlockSpec((tm, hq), row),
                   pl.BlockSpec((None, None, hv, tm), lambda i: (i // ns, i % ns, 0, 0))],
        out_shape=[jax.ShapeDtypeStruct((batch, hq, seq), BF16),
                   jax.ShapeDtypeStruct((t, hq), BF16),
                   jax.ShapeDtypeStruct((batch, ns, hv, tm), BF16)],
        compiler_params=_params("parallel"),
        name="mla_prep",
    )(proj_small, rope_c, rope_sa, rope_sb, q_norm.reshape(1, -1), kv_norm.reshape(1, -1), wq, wkv)


def _mla_attn_kernel(qt_ref, k_ref, vt_ref, o_ref, *, tq):
    qi = pl.program_id(1)
    H = MLA_HEADS
    qt = qt_ref[...].reshape(H, MLA_QK, tq)

    def scores(j):
        start = pl.multiple_of(j * tq, tq)
        kk = k_ref[pl.ds(start, tq), :]
        kh = jnp.stack([kk[:, h * MLA_QK:(h + 1) * MLA_QK] for h in range(H)])
        return _bdot(kh, qt)

    def update(carry, s, vtj):
        m, acc = carry
        m_new = jnp.maximum(m, jnp.max(s, axis=1, keepdims=True))
        alpha = jnp.exp2(m - m_new)
        p = jnp.exp2(s - m_new)
        acc = alpha * acc + _bdot(vtj.reshape(H, MLA_VT_ROWS, tq), p.astype(BF16))
        return m_new, acc

    def body(j, carry):
        return update(carry, scores(j), vt_ref[j])

    init = (jnp.full((H, 1, tq), -jnp.inf, F32), jnp.zeros((H, MLA_VT_ROWS, tq), F32))
    carry = lax.fori_loop(0, qi, body, init)
    chunk_bits = CHUNK.bit_length() - 1
    kc = lax.shift_right_logical(lax.broadcasted_iota(jnp.int32, (tq, tq), 0), chunk_bits)
    qc = lax.shift_right_logical(lax.broadcasted_iota(jnp.int32, (tq, tq), 1), chunk_bits)
    s = jnp.where((kc <= qc)[None], scores(qi), -jnp.inf)
    _, acc = update(carry, s, vt_ref[qi])
    o = acc[:, 0:MLA_V] / acc[:, MLA_V:MLA_V + 1]
    for h in range(H):
        o_ref[:, h * MLA_V:(h + 1) * MLA_V] = o[h].T.astype(o_ref.dtype)


def mla_attention(qt, k, vt, batch, seq, tq):
    t = batch * seq
    nq = seq // tq
    hq = MLA_HEADS * MLA_QK
    hv = MLA_HEADS * MLA_V
    hvt = MLA_HEADS * MLA_VT_ROWS
    return pl.pallas_call(
        functools.partial(_mla_attn_kernel, tq=tq),
        grid=(batch, nq),
        in_specs=[pl.BlockSpec((None, hq, tq), lambda b, i: (b, 0, i)),
                  pl.BlockSpec((seq, hq), lambda b, i: (b, 0)),
                  pl.BlockSpec((None, nq, hvt, tq), lambda b, i: (b, 0, 0, 0))],
        out_specs=pl.BlockSpec((tq, hv), lambda b, i: (b * nq + i, 0)),
        out_shape=jax.ShapeDtypeStruct((t, hv), BF16),
        compiler_params=_params("parallel", "parallel"),
        name="mla_attention",
    )(qt, k, vt)


def _xattn_kernel(q_ref, k_ref, v_ref, o_ref):
    scale = XA_HEAD_DIM ** -0.5
    for h in range(XA_HEADS):
        sl = slice(h * XA_HEAD_DIM, (h + 1) * XA_HEAD_DIM)
        s = _dot_nt(q_ref[:, sl], k_ref[:, sl]) * scale
        p = jnp.exp(s - jnp.max(s, axis=-1, keepdims=True))
        l = jnp.sum(p, axis=-1, keepdims=True)
        o_ref[:, sl] = (_dot(p.astype(BF16), v_ref[:, sl]) / l).astype(o_ref.dtype)


def cross_attention(q, k, v, batch, seq, n_mem, tq=512):
    t, d = q.shape
    nq = seq // tq
    return pl.pallas_call(
        _xattn_kernel,
        grid=(batch, nq),
        in_specs=[pl.BlockSpec((tq, d), lambda b, i: (b * nq + i, 0)),
                  pl.BlockSpec((n_mem, d), lambda b, i: (b, 0)),
                  pl.BlockSpec((n_mem, d), lambda b, i: (b, 0))],
        out_specs=pl.BlockSpec((tq, d), lambda b, i: (b * nq + i, 0)),
        out_shape=jax.ShapeDtypeStruct((t, d), BF16),
        compiler_params=_params("parallel", "parallel"),
        name="cross_attention",
    )(q, k, v)


def _ffn_up_kernel(a_ref, wg_ref, wu_ref, cg_ref, cu_ref, bg_ref, bu_ref, o_ref, eg_ref, eu_ref, wg16_ref, wu16_ref,
                   *, tiles_per_seq):
    i = pl.program_id(1)
    tm = a_ref.shape[0]

    @pl.when(i == 0)
    def _():
        wg16_ref[...] = wg_ref[...].astype(BF16)
        wu16_ref[...] = wu_ref[...].astype(BF16)

    @pl.when(i % tiles_per_seq == 0)
    def _():
        eg_ref[0:SUBLANES, :] = jnp.zeros((SUBLANES, eg_ref.shape[1]), F32)
        eu_ref[0:SUBLANES, :] = jnp.zeros((SUBLANES, eu_ref.shape[1]), F32)

    a = a_ref[...]

    def conv(w_ref, cw_ref, b_ref, e_ref):
        acc = _dot(a, w_ref[...])
        e_ref[SUBLANES:SUBLANES + tm, :] = acc
        cw = cw_ref[...]
        y = cw[FFN_CONV - 1:FFN_CONV, :] * acc + b_ref[...]
        for j in range(FFN_CONV - 1):
            shift = FFN_CONV - 1 - j
            y = y + cw[j:j + 1, :] * e_ref[SUBLANES - shift:SUBLANES - shift + tm, :]
        e_ref[0:SUBLANES, :] = acc[tm - SUBLANES:tm, :]
        return y

    gate = conv(wg16_ref, cg_ref, bg_ref, eg_ref)
    up = conv(wu16_ref, cu_ref, bu_ref, eu_ref)
    o_ref[...] = (_silu(gate) * up).astype(o_ref.dtype)


def ffn_up(a, w_up, layer, conv_w, conv_b, seq, tm=1024, tn=512):
    m, k = a.shape
    nb = D_FF // tn
    return pl.pallas_call(
        functools.partial(_ffn_up_kernel, tiles_per_seq=seq // tm),
        grid=(nb, m // tm),
        in_specs=[pl.BlockSpec((tm, k), lambda j, i: (i, 0)),
                  _weight_spec(w_up, layer, (k, tn), lambda j, i: (0, j)),
                  _weight_spec(w_up, layer, (k, tn), lambda j, i: (0, j + nb)),
                  pl.BlockSpec((FFN_CONV, tn), lambda j, i: (0, j)),
                  pl.BlockSpec((FFN_CONV, tn), lambda j, i: (0, j + nb)),
                  pl.BlockSpec((1, tn), lambda j, i: (0, j)),
                  pl.BlockSpec((1, tn), lambda j, i: (0, j + nb))],
        out_specs=pl.BlockSpec((tm, tn), lambda j, i: (i, j)),
        out_shape=jax.ShapeDtypeStruct((m, D_FF), BF16),
        scratch_shapes=[pltpu.VMEM((SUBLANES + tm, tn), F32), pltpu.VMEM((SUBLANES + tm, tn), F32),
                        pltpu.VMEM((k, tn), BF16), pltpu.VMEM((k, tn), BF16)],
        compiler_params=_params("parallel", "arbitrary"),
        name="ffn_up",
    )(a, w_up, w_up, conv_w, conv_w, conv_b.reshape(1, -1), conv_b.reshape(1, -1))


def _prep_in_small(w_in_t, layer):
    d = w_in_t.shape[2]
    base = 4 * DN_WIDTH
    rows = lambda a, n: lax.slice(w_in_t, (layer, a, 0), (layer + 1, a + n, d)).reshape(n, d)
    wb = rows(base, DN_HEADS)
    wa = rows(base + DN_HEADS, DN_HEADS)
    mq0 = base + 2 * DN_HEADS
    wq = rows(mq0, MLA_Q_RANK)
    wckv = rows(mq0 + MLA_Q_RANK, MLA_KV_RANK)
    wkpe = rows(mq0 + MLA_Q_RANK + MLA_KV_RANK, MLA_ROPE)
    z = lambda n: jnp.zeros((n, d), w_in_t.dtype)
    return jnp.concatenate([wq, wckv, wkpe, z(LANES - MLA_ROPE), wb, wa, z(LANES - 2 * DN_HEADS)], axis=0)


def _prep_wqb(w):
    r = w.shape[0]
    w3 = w.reshape(r, MLA_HEADS, MLA_NOPE + MLA_ROPE)
    w3 = jnp.concatenate([w3, jnp.zeros((r, MLA_HEADS, LANES - MLA_ROPE), w.dtype)], axis=2)
    return w3.reshape(r, MLA_HEADS * MLA_QK)


def kernel(x, mem, positions, norm_mix, w_in, dn_conv, dn_a_log, dn_dt_bias, dn_out_norm, mla_q_norm,
           mla_w_qb, mla_kv_norm, mla_w_kvb, w_out, mem_norm, norm_xattn, xa_wq, xa_wk, xa_wv, xa_wo,
           norm_ffn, ffn_w_up, ffn_conv, ffn_conv_bias, ffn_w_down, norm_final):
    batch, seq, d = x.shape
    n_mem = mem.shape[1]
    t = batch * seq
    depth = w_in.shape[0]

    rope_c, rope_sa, rope_sb = rope_tables(positions)
    mem_n = rmsnorm_rows(mem.reshape(batch * n_mem, d), mem_norm, BF16, tm=256)
    h = x.reshape(t, d)
    u = rmsnorm_rows(h, norm_mix[0], BF16)

    w_in_t = jnp.swapaxes(w_in, 1, 2)
    w_down16 = ffn_w_down.astype(BF16)

    for l in range(depth):
        proj_main = matmul(u, w_in_t, F32, tm=1024, tn=1024, name="in_proj_main", layer=l, n_cols=4 * DN_WIDTH,
                           w_transposed=True)
        proj_small = matmul(u, _prep_in_small(w_in_t, l), F32, tm=1024, tn=SM_WIDTH, name="in_proj_small",
                            w_transposed=True)

        o_dn = deltanet(proj_main, proj_small, dn_conv[l], dn_a_log[l], dn_dt_bias[l], dn_out_norm[l], batch, seq)

        qt, k, vt = mla_prep(proj_small, rope_c, rope_sa, rope_sb, mla_q_norm[l], mla_kv_norm[l],
                             _prep_wqb(mla_w_qb[l]).astype(BF16), mla_w_kvb[l].astype(BF16),
                             batch, seq, tm=MLA_TILE)
        o_mla = mla_attention(qt, k, vt, batch, seq, tq=MLA_TILE)

        h, u = matmul_residual_norm([o_dn, o_mla], w_out, h, norm_xattn[l], BF16, tm=256, name="out_proj", layer=l)

        xq = matmul(u, xa_wq, BF16, tm=1024, tn=1024, name="xa_q", layer=l)
        xk = matmul(mem_n, xa_wk, BF16, tm=1024, tn=1024, name="xa_k", layer=l)
        xv = matmul(mem_n, xa_wv, BF16, tm=1024, tn=1024, name="xa_v", layer=l)
        o_xa = cross_attention(xq, xk, xv, batch, seq, n_mem)
        h, u = matmul_residual_norm([o_xa], xa_wo, h, norm_ffn[l], BF16, tm=256, name="xa_o", layer=l)

        hid = ffn_up(u, ffn_w_up, l, ffn_conv[l], ffn_conv_bias[l], seq)
        if l + 1 < depth:
            h, u = matmul_residual_norm([hid], w_down16, h, norm_mix[l + 1], BF16, tm=256, name="ffn_down", layer=l)
        else:
            out = matmul_residual_norm([hid], w_down16, h, norm_final, x.dtype, tm=256, emit_h=False,
                                       name="ffn_down_final", layer=l)
    return out.reshape(batch, seq, d)
```

```python
import functools

import numpy as np
import jax
import jax.numpy as jnp
from jax import lax
from jax.experimental import pallas as pl
from jax.experimental.pallas import tpu as pltpu

F32 = jnp.float32
BF16 = jnp.bfloat16

D_MODEL = 2048
DEPTH = 4
CHUNK = 64
DN_HEADS = 8
DN_HEAD_DIM = 128
DN_WIDTH = DN_HEADS * DN_HEAD_DIM
DN_CONV = 4
MLA_HEADS = 8
MLA_NOPE = 128
MLA_ROPE = 64
MLA_V = 128
MLA_Q_RANK = 512
MLA_KV_RANK = 256
ROPE_BASE = 10000.0
XA_HEADS = 4
XA_HEAD_DIM = D_MODEL // XA_HEADS
D_FF = 5632
FFN_CONV = 3
EPS = 1e-6
LOG2_E = float(np.log2(np.e))

LANES = 128
SUBLANES = 8
MXU_COLS = 256
VMEM_LIMIT = 56 * 1024 * 1024

SM_Q = 0
SM_CKV = MLA_Q_RANK
SM_KPE = SM_CKV + MLA_KV_RANK
SM_BA = SM_KPE + LANES
SM_WIDTH = SM_BA + LANES
MLA_QK = MLA_NOPE + LANES
MLA_VT_ROWS = MLA_V + 16
MLA_TILE = 512
DN_CHUNKS_PER_STEP = 4


def _params(*sem):
    return pltpu.CompilerParams(dimension_semantics=sem, vmem_limit_bytes=VMEM_LIMIT)


def _dot(a, b):
    return jnp.dot(a, b, preferred_element_type=F32)


def _dot_nt(a, b):
    return lax.dot_general(a, b, (((1,), (1,)), ((), ())), preferred_element_type=F32)


def _bdot(a, b):
    return jnp.einsum("hmk,hkn->hmn", a, b, preferred_element_type=F32)


def _bdot_nt(a, b):
    return jnp.einsum("hmk,hnk->hmn", a, b, preferred_element_type=F32)


def _rms(x, gain):
    return x * lax.rsqrt(jnp.mean(x * x, axis=-1, keepdims=True) + EPS) * gain


def _silu(x):
    return x * jax.nn.sigmoid(x)


def _rmsnorm_kernel(x_ref, g_ref, o_ref):
    o_ref[...] = _rms(x_ref[...].astype(F32), g_ref[...]).astype(o_ref.dtype)


def rmsnorm_rows(x, gain, out_dtype, tm=512):
    m, d = x.shape
    return pl.pallas_call(
        _rmsnorm_kernel,
        grid=(m // tm,),
        in_specs=[pl.BlockSpec((tm, d), lambda i: (i, 0)),
                  pl.BlockSpec((1, d), lambda i: (0, 0))],
        out_specs=pl.BlockSpec((tm, d), lambda i: (i, 0)),
        out_shape=jax.ShapeDtypeStruct((m, d), out_dtype),
        compiler_params=_params("parallel"),
        name="rmsnorm_rows",
    )(x, gain.reshape(1, d))


def _weight_spec(w, layer, block, index_fn, **kw):
    if layer is None:
        return pl.BlockSpec(block, index_fn, **kw)
    return pl.BlockSpec((None,) + block, lambda *g: (layer,) + index_fn(*g), **kw)


def _mm_kernel(a_ref, w_ref, o_ref, *scratch, w_transposed):
    if scratch:
        w16_ref, = scratch

        @pl.when(pl.program_id(1) == 0)
        def _():
            if w_transposed:
                for c in range(0, w_ref.shape[0], MXU_COLS):
                    w16_ref[:, c:c + MXU_COLS] = w_ref[c:c + MXU_COLS, :].T.astype(BF16)
            else:
                w16_ref[...] = w_ref[...].astype(BF16)

        w = w16_ref[...]
    else:
        w = w_ref[...]
    o_ref[...] = _dot(a_ref[...], w).astype(o_ref.dtype)


def matmul(a, w, out_dtype, tm, tn, name, layer=None, n_cols=None, w_transposed=False):
    m, k = a.shape
    cast = w.dtype != BF16
    assert cast or not w_transposed
    n = w.shape[-2 if w_transposed else -1] if n_cols is None else n_cols
    if w_transposed:
        w_spec = _weight_spec(w, layer, (tn, k), lambda j, i: (j, 0))
    else:
        w_spec = _weight_spec(w, layer, (k, tn), lambda j, i: (0, j))
    return pl.pallas_call(
        functools.partial(_mm_kernel, w_transposed=w_transposed),
        grid=(n // tn, m // tm),
        in_specs=[pl.BlockSpec((tm, k), lambda j, i: (i, 0)), w_spec],
        out_specs=pl.BlockSpec((tm, tn), lambda j, i: (i, j)),
        out_shape=jax.ShapeDtypeStruct((m, n), out_dtype),
        scratch_shapes=[pltpu.VMEM((k, tn), BF16)] if cast else [],
        compiler_params=_params("parallel", "arbitrary"),
        name=name,
    )(a, w)


def _mm_res_norm_kernel(*refs, n_parts, emit_h, cast):
    a_refs = refs[:n_parts]
    w_ref, h_ref, g_ref = refs[n_parts:n_parts + 3]
    outs = refs[n_parts + 3:]
    ho_ref = outs[0] if emit_h else None
    uo_ref = outs[1] if emit_h else outs[0]
    if cast:
        w16_ref = outs[-1]

        @pl.when(pl.program_id(0) == 0)
        def _():
            w16_ref[...] = w_ref[...].astype(BF16)

        w_ref = w16_ref
    off = 0
    acc = None
    for a_ref in a_refs:
        kp = a_ref.shape[1]
        part = _dot(a_ref[...], w_ref[off:off + kp, :])
        acc = part if acc is None else acc + part
        off += kp
    h = h_ref[...] + acc
    if emit_h:
        ho_ref[...] = h
    uo_ref[...] = _rms(h, g_ref[...]).astype(uo_ref.dtype)


def matmul_residual_norm(a_parts, w, h, gain, u_dtype, tm, name, layer=None, emit_h=True):
    m, n = h.shape
    k = w.shape[-2]
    n_parts = len(a_parts)
    cast = w.dtype != BF16
    row = lambda i: (i, 0)
    in_specs = [pl.BlockSpec((tm, a.shape[1]), row) for a in a_parts] + [
        _weight_spec(w, layer, (k, n), lambda i: (0, 0), pipeline_mode=pl.Buffered(1)),
        pl.BlockSpec((tm, n), row),
        pl.BlockSpec((1, n), lambda i: (0, 0)),
    ]
    out_specs = [pl.BlockSpec((tm, n), row)]
    out_shape = [jax.ShapeDtypeStruct((m, n), u_dtype)]
    if emit_h:
        out_specs = [pl.BlockSpec((tm, n), row)] + out_specs
        out_shape = [jax.ShapeDtypeStruct((m, n), F32)] + out_shape
    res = pl.pallas_call(
        functools.partial(_mm_res_norm_kernel, n_parts=n_parts, emit_h=emit_h, cast=cast),
        grid=(m // tm,),
        in_specs=in_specs,
        out_specs=out_specs,
        out_shape=out_shape,
        scratch_shapes=[pltpu.VMEM((k, n), BF16)] if cast else [],
        compiler_params=_params("arbitrary"),
        name=name,
    )(*a_parts, w, h, gain.reshape(1, n))
    return tuple(res) if emit_h else res[0]


def _rope_table_kernel(pos_ref, inv_ref, c_ref, sa_ref, sb_ref):
    ang = pos_ref[...].astype(F32) * inv_ref[...]
    c = jnp.cos(ang)
    s = jnp.sin(ang)
    lane = lax.broadcasted_iota(jnp.int32, ang.shape, 1)
    half = MLA_ROPE // 2
    c_ref[...] = jnp.where(lane < MLA_ROPE, c, 0.0)
    sa_ref[...] = jnp.where(lane < half, -s, 0.0)
    sb_ref[...] = jnp.where((lane >= half) & (lane < MLA_ROPE), s, 0.0)


def rope_tables(positions, tm=512):
    t = positions.size
    inv = ROPE_BASE ** (-jnp.arange(0, MLA_ROPE, 2, dtype=F32) / MLA_ROPE)
    inv_row = jnp.concatenate([inv, inv, jnp.zeros((LANES - MLA_ROPE,), F32)]).reshape(1, LANES)
    spec = pl.BlockSpec((tm, LANES), lambda i: (i, 0))
    shape = jax.ShapeDtypeStruct((t, LANES), F32)
    return pl.pallas_call(
        _rope_table_kernel,
        grid=(t // tm,),
        in_specs=[pl.BlockSpec((tm, 1), lambda i: (i, 0)),
                  pl.BlockSpec((1, LANES), lambda i: (0, 0))],
        out_specs=[spec, spec, spec],
        out_shape=[shape, shape, shape],
        compiler_params=_params("parallel"),
        name="rope_tables",
    )(positions.reshape(t, 1), inv_row)


def _rope(x, c, sa, sb):
    half = MLA_ROPE // 2
    return x * c + pltpu.roll(x, LANES - half, 1) * sa + pltpu.roll(x, half, 1) * sb


def _deltanet_kernel(x_ref, ba_ref, cw_ref, alog_ref, dtb_ref, onorm_ref, o_ref, xs_ref, state_ref, *, nch):
    C, D, H = CHUNK, DN_HEAD_DIM, DN_HEADS
    W3 = 3 * DN_WIDTH
    TS = nch * C
    pairs = [(c, h) for c in range(nch) for h in range(H)]

    @pl.when(pl.program_id(1) == 0)
    def _():
        xs_ref[0:SUBLANES, :] = jnp.zeros((SUBLANES, W3), F32)
        state_ref[...] = jnp.zeros_like(state_ref)

    x = x_ref[:, 0:W3]
    xs_ref[SUBLANES:SUBLANES + TS, :] = x
    cw = cw_ref[...]
    y = cw[DN_CONV - 1:DN_CONV, :] * x
    for j in range(DN_CONV - 1):
        shift = DN_CONV - 1 - j
        y = y + cw[j:j + 1, :] * xs_ref[SUBLANES - shift:SUBLANES - shift + TS, :]
    xs_ref[0:SUBLANES, :] = x[TS - SUBLANES:TS, :]
    y = _silu(y)

    ba = ba_ref[...]
    beta_all = jax.nn.sigmoid(ba)
    sp_in = ba + dtb_ref[...]
    softplus = jnp.maximum(sp_in, 0.0) + jnp.log1p(jnp.exp(-jnp.abs(sp_in)))
    g_all = -jnp.exp(alog_ref[...]) * softplus
    row_in_chunk = lax.broadcasted_iota(jnp.int32, (TS, LANES), 0) & (C - 1)
    G = g_all
    step = 1
    while step < C:
        G = G + jnp.where(row_in_chunk >= step, pltpu.roll(G, step, 0), 0.0)
        step *= 2
    Gc = [G[c * C:(c + 1) * C, :] for c in range(nch)]
    GTc = [g.T for g in Gc]
    g_last = [g[C - 1:C, :] for g in Gc]
    e_g = jnp.exp(G)
    e_rem = [jnp.exp(g_last[c] - Gc[c]) for c in range(nch)]
    e_last = [jnp.exp(g) for g in g_last]

    ri = lax.broadcasted_iota(jnp.int32, (C, C), 0)
    ci = lax.broadcasted_iota(jnp.int32, (C, C), 1)
    incl = (ri >= ci)[None]
    strict = (ri > ci)[None]
    eye = jnp.where(ri == ci, 1.0, 0.0).astype(F32)[None]
    onorm = onorm_ref[...]

    def heads(off):
        return jnp.stack([y[c * C:(c + 1) * C, off + h * D:off + (h + 1) * D] for c, h in pairs])

    def head_cols(a, lane0):
        return jnp.stack([a[c * C:(c + 1) * C, lane0 + h:lane0 + h + 1] for c, h in pairs])

    q = heads(0)
    k = heads(DN_WIDTH)
    v = heads(2 * DN_WIDTH)
    q = q * lax.rsqrt(jnp.sum(q * q, axis=-1, keepdims=True) + EPS) * (D ** -0.5)
    k = k * lax.rsqrt(jnp.sum(k * k, axis=-1, keepdims=True) + EPS)
    beta = head_cols(beta_all, 0)
    g_col = head_cols(G, H)
    g_row = jnp.stack([GTc[c][H + h:H + h + 1, :] for c, h in pairs])
    eg_col = head_cols(e_g, H)
    erem_col = jnp.stack([e_rem[c][:, H + h:H + h + 1] for c, h in pairs])
    elast = jnp.stack([e_last[c][:, H + h:H + h + 1] for c, h in pairs])
    decay = jnp.exp(jnp.where(incl, g_col - g_row, -jnp.inf))
    kb = k * beta
    k16 = k.astype(BF16)
    kq = _bdot_nt(jnp.concatenate([kb.astype(BF16), q.astype(BF16)], axis=1), k16)
    lower = jnp.where(strict, kq[:, 0:C] * decay, 0.0)
    attn16 = (kq[:, C:2 * C] * decay).astype(BF16)

    def same_block(bits):
        return (ri >> bits) == (ci >> bits)

    base_bits = 3
    ld = jnp.where((strict[0] & same_block(base_bits))[None], lower, 0.0)
    ld16 = ld.astype(BF16)
    t_inv = eye - ld
    p16 = _bdot(ld16, ld16).astype(BF16)
    r = _bdot(jnp.concatenate([t_inv.astype(BF16), p16], axis=1), p16)
    t_inv = t_inv + r[:, 0:C]
    t_inv = t_inv + _bdot(t_inv.astype(BF16), r[:, C:2 * C].astype(BF16))
    for bits in range(base_bits, C.bit_length() - 1):
        off_diag = (strict[0] & same_block(bits + 1) & jnp.logical_not(same_block(bits)))[None]
        t16 = t_inv.astype(BF16)
        x = _bdot(t16, jnp.where(off_diag, lower, 0.0).astype(BF16))
        t_inv = t_inv - _bdot(x.astype(BF16), t16)

    rhs = jnp.concatenate([v * beta, kb * eg_col], axis=2)
    sol = _bdot(t_inv.astype(BF16), rhs.astype(BF16))
    u = sol[:, :, 0:D]
    wq16 = jnp.concatenate([sol[:, :, D:2 * D].astype(BF16), (q * eg_col).astype(BF16)], axis=1)
    kdec16 = (k * erem_col).astype(BF16)

    s = state_ref[...]
    for c in range(nch):
        sl = slice(c * H, (c + 1) * H)
        ws = _bdot(wq16[sl], s.astype(BF16))
        v_new = u[sl] - ws[:, 0:C]
        vn16 = v_new.astype(BF16)
        o = ws[:, C:2 * C] + _bdot(attn16[sl], vn16)
        s = s * elast[sl] + jnp.einsum("hck,hcv->hkv", kdec16[sl], vn16, preferred_element_type=F32)
        on = _rms(o, onorm[None])
        for h in range(H):
            z = x_ref[c * C:(c + 1) * C, W3 + h * D:W3 + (h + 1) * D]
            o_ref[c * C:(c + 1) * C, h * D:(h + 1) * D] = (on[h] * _silu(z)).astype(o_ref.dtype)
    state_ref[...] = s


def deltanet(proj_main, proj_small, conv_w, a_log, dt_bias, out_norm, batch, seq, nch=DN_CHUNKS_PER_STEP):
    t = batch * seq
    ts = nch * CHUNK
    ns = seq // ts
    pad = jnp.zeros((LANES - 2 * DN_HEADS,), F32)
    alog_row = jnp.concatenate([jnp.zeros((DN_HEADS,), F32), a_log, pad]).reshape(1, LANES)
    dtb_row = jnp.concatenate([jnp.zeros((DN_HEADS,), F32), dt_bias, pad]).reshape(1, LANES)
    w4 = 4 * DN_WIDTH
    return pl.pallas_call(
        functools.partial(_deltanet_kernel, nch=nch),
        grid=(batch, ns),
        in_specs=[pl.BlockSpec((ts, w4), lambda b, c: (b * ns + c, 0)),
                  pl.BlockSpec((ts, LANES), lambda b, c: (b * ns + c, SM_BA // LANES)),
                  pl.BlockSpec((DN_CONV, 3 * DN_WIDTH), lambda b, c: (0, 0)),
                  pl.BlockSpec((1, LANES), lambda b, c: (0, 0)),
                  pl.BlockSpec((1, LANES), lambda b, c: (0, 0)),
                  pl.BlockSpec((1, DN_HEAD_DIM), lambda b, c: (0, 0))],
        out_specs=pl.BlockSpec((ts, DN_WIDTH), lambda b, c: (b * ns + c, 0)),
        out_shape=jax.ShapeDtypeStruct((t, DN_WIDTH), BF16),
        scratch_shapes=[pltpu.VMEM((SUBLANES + ts, 3 * DN_WIDTH), F32),
                        pltpu.VMEM((DN_HEADS, DN_HEAD_DIM, DN_HEAD_DIM), F32)],
        compiler_params=_params("arbitrary", "arbitrary"),
        name="deltanet",
    )(proj_main, proj_small, conv_w, alog_row, dtb_row, out_norm.reshape(1, DN_HEAD_DIM))


def _mla_prep_kernel(x_ref, c_ref, sa_ref, sb_ref, qn_ref, kvn_ref, wq_ref, wkv_ref, qt_ref, k_ref, vt_ref):
    c, sa, sb = c_ref[...], sa_ref[...], sb_ref[...]
    scale = (MLA_NOPE + MLA_ROPE) ** -0.5 * LOG2_E
    q_lat = _rms(x_ref[:, SM_Q:SM_Q + MLA_Q_RANK], qn_ref[...])
    qf = _dot(q_lat.astype(BF16), wq_ref[...]) * scale
    c_kv = _rms(x_ref[:, SM_CKV:SM_CKV + MLA_KV_RANK], kvn_ref[...])
    kvf = _dot(c_kv.astype(BF16), wkv_ref[...])
    k_pe = _rope(x_ref[:, SM_KPE:SM_KPE + LANES], c, sa, sb).astype(k_ref.dtype)
    pad_rows = MLA_VT_ROWS - MLA_V
    ones_row = (lax.broadcasted_iota(jnp.int32, (pad_rows, x_ref.shape[0]), 0) == 0).astype(vt_ref.dtype)
    for h in range(MLA_HEADS):
        o = h * MLA_QK
        q_h = jnp.concatenate([qf[:, o:o + MLA_NOPE], _rope(qf[:, o + MLA_NOPE:o + MLA_QK], c, sa, sb)], axis=1)
        qt_ref[o:o + MLA_QK, :] = q_h.T.astype(qt_ref.dtype)
        k_ref[:, o:o + MLA_NOPE] = kvf[:, o:o + MLA_NOPE].astype(k_ref.dtype)
        k_ref[:, o + MLA_NOPE:o + MLA_QK] = k_pe
        vt_ref[h * MLA_VT_ROWS:h * MLA_VT_ROWS + MLA_V, :] = kvf[:, o + MLA_NOPE:o + MLA_QK].T.astype(vt_ref.dtype)
        vt_ref[h * MLA_VT_ROWS + MLA_V:(h + 1) * MLA_VT_ROWS, :] = ones_row


def mla_prep(proj_small, rope_c, rope_sa, rope_sb, q_norm, kv_norm, wq, wkv, batch, seq, tm):
    t = proj_small.shape[0]
    ns = seq // tm
    hq = MLA_HEADS * MLA_QK
    hv = MLA_HEADS * MLA_VT_ROWS
    row = lambda i: (i, 0)
    fixed = lambda i: (0, 0)
    return pl.pallas_call(
        _mla_prep_kernel,
        grid=(t // tm,),
        in_specs=[pl.BlockSpec((tm, SM_WIDTH), row),
                  pl.BlockSpec((tm, LANES), row), pl.BlockSpec((tm, LANES), row), pl.BlockSpec((tm, LANES), row),
                  pl.BlockSpec((1, MLA_Q_RANK), fixed), pl.BlockSpec((1, MLA_KV_RANK), fixed),
                  pl.BlockSpec((MLA_Q_RANK, hq), fixed), pl.BlockSpec((MLA_KV_RANK, hq), fixed)],
        out_specs=[pl.BlockSpec((None, hq, tm), lambda i: (i // ns, 0, i % ns)),
                   pl.BlockSpec((tm, hq), row),
                   pl.BlockSpec((None, None, hv, tm), lambda i: (i // ns, i % ns, 0, 0))],
        out_shape=[jax.ShapeDtypeStruct((batch, hq, seq), BF16),
                   jax.ShapeDtypeStruct((t, hq), BF16),
                   jax.ShapeDtypeStruct((batch, ns, hv, tm), BF16)],
        compiler_params=_params("parallel"),
        name="mla_prep",
    )(proj_small, rope_c, rope_sa, rope_sb, q_norm.reshape(1, -1), kv_norm.reshape(1, -1), wq, wkv)


def _mla_attn_kernel(qt_ref, k_ref, vt_ref, o_ref, *, tq):
    qi = pl.program_id(1)
    H = MLA_HEADS
    qt = qt_ref[...].reshape(H, MLA_QK, tq)

    def scores(j):
        start = pl.multiple_of(j * tq, tq)
        kk = k_ref[pl.ds(start, tq), :]
        kh = jnp.stack([kk[:, h * MLA_QK:(h + 1) * MLA_QK] for h in range(H)])
        return _bdot(kh, qt)

    def update(carry, s, vtj):
        m, acc = carry
        m_new = jnp.maximum(m, jnp.max(s, axis=1, keepdims=True))
        alpha = jnp.exp2(m - m_new)
        p = jnp.exp2(s - m_new)
        acc = alpha * acc + _bdot(vtj.reshape(H, MLA_VT_ROWS, tq), p.astype(BF16))
        return m_new, acc

    def body(j, carry):
        return update(carry, scores(j), vt_ref[j])

    init = (jnp.full((H, 1, tq), -jnp.inf, F32), jnp.zeros((H, MLA_VT_ROWS, tq), F32))
    carry = lax.fori_loop(0, qi, body, init)
    chunk_bits = CHUNK.bit_length() - 1
    kc = lax.shift_right_logical(lax.broadcasted_iota(jnp.int32, (tq, tq), 0), chunk_bits)
    qc = lax.shift_right_logical(lax.broadcasted_iota(jnp.int32, (tq, tq), 1), chunk_bits)
    s = jnp.where((kc <= qc)[None], scores(qi), -jnp.inf)
    _, acc = update(carry, s, vt_ref[qi])
    o = acc[:, 0:MLA_V] / acc[:, MLA_V:MLA_V + 1]
    for h in range(H):
        o_ref[:, h * MLA_V:(h + 1) * MLA_V] = o[h].T.astype(o_ref.dtype)


def mla_attention(qt, k, vt, batch, seq, tq):
    t = batch * seq
    nq = seq // tq
    hq = MLA_HEADS * MLA_QK
    hv = MLA_HEADS * MLA_V
    hvt = MLA_HEADS * MLA_VT_ROWS
    return pl.pallas_call(
        functools.partial(_mla_attn_kernel, tq=tq),
        grid=(batch, nq),
        in_specs=[pl.BlockSpec((None, hq, tq), lambda b, i: (b, 0, i)),
                  pl.BlockSpec((seq, hq), lambda b, i: (b, 0)),
                  pl.BlockSpec((None, nq, hvt, tq), lambda b, i: (b, 0, 0, 0))],
        out_specs=pl.BlockSpec((tq, hv), lambda b, i: (b * nq + i, 0)),
        out_shape=jax.ShapeDtypeStruct((t, hv), BF16),
        compiler_params=_params("parallel", "parallel"),
        name="mla_attention",
    )(qt, k, vt)


def _xattn_kernel(q_ref, k_ref, v_ref, o_ref):
    scale = XA_HEAD_DIM ** -0.5
    for h in range(XA_HEADS):
        sl = slice(h * XA_HEAD_DIM, (h + 1) * XA_HEAD_DIM)
        s = _dot_nt(q_ref[:, sl], k_ref[:, sl]) * scale
        p = jnp.exp(s - jnp.max(s, axis=-1, keepdims=True))
        l = jnp.sum(p, axis=-1, keepdims=True)
        o_ref[:, sl] = (_dot(p.astype(BF16), v_ref[:, sl]) / l).astype(o_ref.dtype)


def cross_attention(q, k, v, batch, seq, n_mem, tq=512):
    t, d = q.shape
    nq = seq // tq
    return pl.pallas_call(
        _xattn_kernel,
        grid=(batch, nq),
        in_specs=[pl.BlockSpec((tq, d), lambda b, i: (b * nq + i, 0)),
                  pl.BlockSpec((n_mem, d), lambda b, i: (b, 0)),
                  pl.BlockSpec((n_mem, d), lambda b, i: (b, 0))],
        out_specs=pl.BlockSpec((tq, d), lambda b, i: (b * nq + i, 0)),
        out_shape=jax.ShapeDtypeStruct((t, d), BF16),
        compiler_params=_params("parallel", "parallel"),
        name="cross_attention",
    )(q, k, v)


def _ffn_up_kernel(a_ref, wg_ref, wu_ref, cg_ref, cu_ref, bg_ref, bu_ref, o_ref, eg_ref, eu_ref, wg16_ref, wu16_ref,
                   *, tiles_per_seq):
    i = pl.program_id(1)
    tm = a_ref.shape[0]

    @pl.when(i == 0)
    def _():
        wg16_ref[...] = wg_ref[...].astype(BF16)
        wu16_ref[...] = wu_ref[...].astype(BF16)

    @pl.when(i % tiles_per_seq == 0)
    def _():
        eg_ref[0:SUBLANES, :] = jnp.zeros((SUBLANES, eg_ref.shape[1]), F32)
        eu_ref[0:SUBLANES, :] = jnp.zeros((SUBLANES, eu_ref.shape[1]), F32)

    a = a_ref[...]

    def conv(w_ref, cw_ref, b_ref, e_ref):
        acc = _dot(a, w_ref[...])
        e_ref[SUBLANES:SUBLANES + tm, :] = acc
        cw = cw_ref[...]
        y = cw[FFN_CONV - 1:FFN_CONV, :] * acc + b_ref[...]
        for j in range(FFN_CONV - 1):
            shift = FFN_CONV - 1 - j
            y = y + cw[j:j + 1, :] * e_ref[SUBLANES - shift:SUBLANES - shift + tm, :]
        e_ref[0:SUBLANES, :] = acc[tm - SUBLANES:tm, :]
        return y

    gate = conv(wg16_ref, cg_ref, bg_ref, eg_ref)
    up = conv(wu16_ref, cu_ref, bu_ref, eu_ref)
    o_ref[...] = (_silu(gate) * up).astype(o_ref.dtype)


def ffn_up(a, w_up, layer, conv_w, conv_b, seq, tm=1024, tn=512):
    m, k = a.shape
    nb = D_FF // tn
    return pl.pallas_call(
        functools.partial(_ffn_up_kernel, tiles_per_seq=seq // tm),
        grid=(nb, m // tm),
        in_specs=[pl.BlockSpec((tm, k), lambda j, i: (i, 0)),
                  _weight_spec(w_up, layer, (k, tn), lambda j, i: (0, j)),
                  _weight_spec(w_up, layer, (k, tn), lambda j, i: (0, j + nb)),
                  pl.BlockSpec((FFN_CONV, tn), lambda j, i: (0, j)),
                  pl.BlockSpec((FFN_CONV, tn), lambda j, i: (0, j + nb)),
                  pl.BlockSpec((1, tn), lambda j, i: (0, j)),
                  pl.BlockSpec((1, tn), lambda j, i: (0, j + nb))],
        out_specs=pl.BlockSpec((tm, tn), lambda j, i: (i, j)),
        out_shape=jax.ShapeDtypeStruct((m, D_FF), BF16),
        scratch_shapes=[pltpu.VMEM((SUBLANES + tm, tn), F32), pltpu.VMEM((SUBLANES + tm, tn), F32),
                        pltpu.VMEM((k, tn), BF16), pltpu.VMEM((k, tn), BF16)],
        compiler_params=_params("parallel", "arbitrary"),
        name="ffn_up",
    )(a, w_up, w_up, conv_w, conv_w, conv_b.reshape(1, -1), conv_b.reshape(1, -1))


def _prep_in_small(w_in_t, layer):
    d = w_in_t.shape[2]
    base = 4 * DN_WIDTH
    rows = lambda a, n: lax.slice(w_in_t, (layer, a, 0), (layer + 1, a + n, d)).reshape(n, d)
    wb = rows(base, DN_HEADS)
    wa = rows(base + DN_HEADS, DN_HEADS)
    mq0 = base + 2 * DN_HEADS
    wq = rows(mq0, MLA_Q_RANK)
    wckv = rows(mq0 + MLA_Q_RANK, MLA_KV_RANK)
    wkpe = rows(mq0 + MLA_Q_RANK + MLA_KV_RANK, MLA_ROPE)
    z = lambda n: jnp.zeros((n, d), w_in_t.dtype)
    return jnp.concatenate([wq, wckv, wkpe, z(LANES - MLA_ROPE), wb, wa, z(LANES - 2 * DN_HEADS)], axis=0)


def _prep_wqb(w):
    r = w.shape[0]
    w3 = w.reshape(r, MLA_HEADS, MLA_NOPE + MLA_ROPE)
    w3 = jnp.concatenate([w3, jnp.zeros((r, MLA_HEADS, LANES - MLA_ROPE), w.dtype)], axis=2)
    return w3.reshape(r, MLA_HEADS * MLA_QK)


def kernel(x, mem, positions, norm_mix, w_in, dn_conv, dn_a_log, dn_dt_bias, dn_out_norm, mla_q_norm,
           mla_w_qb, mla_kv_norm, mla_w_kvb, w_out, mem_norm, norm_xattn, xa_wq, xa_wk, xa_wv, xa_wo,
           norm_ffn, ffn_w_up, ffn_conv, ffn_conv_bias, ffn_w_down, norm_final):
    batch, seq, d = x.shape
    n_mem = mem.shape[1]
    t = batch * seq
    depth = w_in.shape[0]

    rope_c, rope_sa, rope_sb = rope_tables(positions)
    mem_n = rmsnorm_rows(mem.reshape(batch * n_mem, d), mem_norm, BF16, tm=256)
    h = x.reshape(t, d)
    u = rmsnorm_rows(h, norm_mix[0], BF16)

    w_in_t = jnp.swapaxes(w_in, 1, 2)
    w_down16 = ffn_w_down.astype(BF16)

    for l in range(depth):
        proj_main = matmul(u, w_in_t, F32, tm=1024, tn=1024, name="in_proj_main", layer=l, n_cols=4 * DN_WIDTH,
                           w_transposed=True)
        proj_small = matmul(u, _prep_in_small(w_in_t, l), F32, tm=1024, tn=SM_WIDTH, name="in_proj_small",
                            w_transposed=True)

        o_dn = deltanet(proj_main, proj_small, dn_conv[l], dn_a_log[l], dn_dt_bias[l], dn_out_norm[l], batch, seq)

        qt, k, vt = mla_prep(proj_small, rope_c, rope_sa, rope_sb, mla_q_norm[l], mla_kv_norm[l],
                             _prep_wqb(mla_w_qb[l]).astype(BF16), mla_w_kvb[l].astype(BF16),
                             batch, seq, tm=MLA_TILE)
        o_mla = mla_attention(qt, k, vt, batch, seq, tq=MLA_TILE)

        h, u = matmul_residual_norm([o_dn, o_mla], w_out, h, norm_xattn[l], BF16, tm=256, name="out_proj", layer=l)

        xq = matmul(u, xa_wq, BF16, tm=1024, tn=1024, name="xa_q", layer=l)
        xk = matmul(mem_n, xa_wk, BF16, tm=1024, tn=1024, name="xa_k", layer=l)
        xv = matmul(mem_n, xa_wv, BF16, tm=1024, tn=1024, name="xa_v", layer=l)
        o_xa = cross_attention(xq, xk, xv, batch, seq, n_mem)
        h, u = matmul_residual_norm([o_xa], xa_wo, h, norm_ffn[l], BF16, tm=256, name="xa_o", layer=l)

        hid = ffn_up(u, ffn_w_up, l, ffn_conv[l], ffn_conv_bias[l], seq)
        if l + 1 < depth:
            h, u = matmul_residual_norm([hid], w_down16, h, norm_mix[l + 1], BF16, tm=256, name="ffn_down", layer=l)
        else:
            out = matmul_residual_norm([hid], w_down16, h, norm_final, x.dtype, tm=256, emit_h=False,
                                       name="ffn_down_final", layer=l)
    return out.reshape(batch, seq, d)
```

```python
import functools

import numpy as np
import jax
import jax.numpy as jnp
from jax import lax
from jax.experimental import pallas as pl
from jax.experimental.pallas import tpu as pltpu

F32 = jnp.float32
BF16 = jnp.bfloat16

D_MODEL = 2048
DEPTH = 4
CHUNK = 64
DN_HEADS = 8
DN_HEAD_DIM = 128
DN_WIDTH = DN_HEADS * DN_HEAD_DIM
DN_CONV = 4
MLA_HEADS = 8
MLA_NOPE = 128
MLA_ROPE = 64
MLA_V = 128
MLA_Q_RANK = 512
MLA_KV_RANK = 256
ROPE_BASE = 10000.0
XA_HEADS = 4
XA_HEAD_DIM = D_MODEL // XA_HEADS
D_FF = 5632
FFN_CONV = 3
EPS = 1e-6
LOG2_E = float(np.log2(np.e))

LANES = 128
SUBLANES = 8
MXU_COLS = 256
VMEM_LIMIT = 56 * 1024 * 1024

SM_Q = 0
SM_CKV = MLA_Q_RANK
SM_KPE = SM_CKV + MLA_KV_RANK
SM_BA = SM_KPE + LANES
SM_WIDTH = SM_BA + LANES
MLA_QK = MLA_NOPE + LANES
MLA_VT_ROWS = MLA_V + 16
MLA_TILE = 512
DN_CHUNKS_PER_STEP = 4


def _params(*sem):
    return pltpu.CompilerParams(dimension_semantics=sem, vmem_limit_bytes=VMEM_LIMIT)


def _dot(a, b):
    return jnp.dot(a, b, preferred_element_type=F32)


def _dot_nt(a, b):
    return lax.dot_general(a, b, (((1,), (1,)), ((), ())), preferred_element_type=F32)


def _bdot(a, b):
    return jnp.einsum("hmk,hkn->hmn", a, b, preferred_element_type=F32)


def _bdot_nt(a, b):
    return jnp.einsum("hmk,hnk->hmn", a, b, preferred_element_type=F32)


def _rms(x, gain):
    return x * lax.rsqrt(jnp.mean(x * x, axis=-1, keepdims=True) + EPS) * gain


def _silu(x):
    return x * jax.nn.sigmoid(x)


def _rmsnorm_kernel(x_ref, g_ref, o_ref):
    o_ref[...] = _rms(x_ref[...].astype(F32), g_ref[...]).astype(o_ref.dtype)


def rmsnorm_rows(x, gain, out_dtype, tm=512):
    m, d = x.shape
    return pl.pallas_call(
        _rmsnorm_kernel,
        grid=(m // tm,),
        in_specs=[pl.BlockSpec((tm, d), lambda i: (i, 0)),
                  pl.BlockSpec((1, d), lambda i: (0, 0))],
        out_specs=pl.BlockSpec((tm, d), lambda i: (i, 0)),
        out_shape=jax.ShapeDtypeStruct((m, d), out_dtype),
        compiler_params=_params("parallel"),
        name="rmsnorm_rows",
    )(x, gain.reshape(1, d))


def _weight_spec(w, layer, block, index_fn, **kw):
    if layer is None:
        return pl.BlockSpec(block, index_fn, **kw)
    return pl.BlockSpec((None,) + block, lambda *g: (layer,) + index_fn(*g), **kw)


def _mm_kernel(a_ref, w_ref, *rest, w_transposed, side_cast):
    if side_cast:
        src_ref, o_ref, dst_ref = rest[:3]
        scratch = rest[3:]
        dst_ref[...] = src_ref[...].astype(dst_ref.dtype)
    else:
        o_ref = rest[0]
        scratch = rest[1:]
    if scratch:
        w16_ref, = scratch

        @pl.when(pl.program_id(1) == 0)
        def _():
            if w_transposed:
                for c in range(0, w_ref.shape[0], MXU_COLS):
                    w16_ref[:, c:c + MXU_COLS] = w_ref[c:c + MXU_COLS, :].T.astype(BF16)
            else:
                w16_ref[...] = w_ref[...].astype(BF16)

        w = w16_ref[...]
    else:
        w = w_ref[...]
    o_ref[...] = _dot(a_ref[...], w).astype(o_ref.dtype)


def matmul(a, w, out_dtype, tm, tn, name, layer=None, n_cols=None, w_transposed=False, side_cast=None):
    m, k = a.shape
    cast = w.dtype != BF16
    assert cast or not w_transposed
    n = w.shape[-2 if w_transposed else -1] if n_cols is None else n_cols
    nm = m // tm
    if w_transposed:
        w_spec = _weight_spec(w, layer, (tn, k), lambda j, i: (j, 0))
    else:
        w_spec = _weight_spec(w, layer, (k, tn), lambda j, i: (0, j))
    in_specs = [pl.BlockSpec((tm, k), lambda j, i: (i, 0)), w_spec]
    out_specs = [pl.BlockSpec((tm, tn), lambda j, i: (i, j))]
    out_shape = [jax.ShapeDtypeStruct((m, n), out_dtype)]
    operands = [a, w]
    if side_cast is not None:
        src, src_layer = side_cast
        rows, cols = src.shape[1] // ((n // tn) * nm), src.shape[2]
        in_specs.append(pl.BlockSpec((None, rows, cols), lambda j, i: (src_layer, j * nm + i, 0)))
        out_specs.append(pl.BlockSpec((rows, cols), lambda j, i: (j * nm + i, 0)))
        out_shape.append(jax.ShapeDtypeStruct(src.shape[1:], BF16))
        operands.append(src)
    res = pl.pallas_call(
        functools.partial(_mm_kernel, w_transposed=w_transposed, side_cast=side_cast is not None),
        grid=(n // tn, nm),
        in_specs=in_specs,
        out_specs=out_specs,
        out_shape=out_shape,
        scratch_shapes=[pltpu.VMEM((k, tn), BF16)] if cast else [],
        compiler_params=_params("parallel", "arbitrary"),
        name=name,
    )(*operands)
    return res[0] if side_cast is None else tuple(res)


def _mm_res_norm_kernel(*refs, n_parts, emit_h, cast):
    a_refs = refs[:n_parts]
    w_ref, h_ref, g_ref = refs[n_parts:n_parts + 3]
    outs = refs[n_parts + 3:]
    ho_ref = outs[0] if emit_h else None
    uo_ref = outs[1] if emit_h else outs[0]
    if cast:
        w16_ref = outs[-1]

        @pl.when(pl.program_id(0) == 0)
        def _():
            w16_ref[...] = w_ref[...].astype(BF16)

        w_ref = w16_ref
    off = 0
    acc = None
    for a_ref in a_refs:
        kp = a_ref.shape[1]
        part = _dot(a_ref[...], w_ref[off:off + kp, :])
        acc = part if acc is None else acc + part
        off += kp
    h = h_ref[...] + acc
    if emit_h:
        ho_ref[...] = h
    uo_ref[...] = _rms(h, g_ref[...]).astype(uo_ref.dtype)


def matmul_residual_norm(a_parts, w, h, gain, u_dtype, tm, name, layer=None, emit_h=True):
    m, n = h.shape
    k = w.shape[-2]
    n_parts = len(a_parts)
    cast = w.dtype != BF16
    row = lambda i: (i, 0)
    in_specs = [pl.BlockSpec((tm, a.shape[1]), row) for a in a_parts] + [
        _weight_spec(w, layer, (k, n), lambda i: (0, 0), pipeline_mode=pl.Buffered(1)),
        pl.BlockSpec((tm, n), row),
        pl.BlockSpec((1, n), lambda i: (0, 0)),
    ]
    out_specs = [pl.BlockSpec((tm, n), row)]
    out_shape = [jax.ShapeDtypeStruct((m, n), u_dtype)]
    if emit_h:
        out_specs = [pl.BlockSpec((tm, n), row)] + out_specs
        out_shape = [jax.ShapeDtypeStruct((m, n), F32)] + out_shape
    res = pl.pallas_call(
        functools.partial(_mm_res_norm_kernel, n_parts=n_parts, emit_h=emit_h, cast=cast),
        grid=(m // tm,),
        in_specs=in_specs,
        out_specs=out_specs,
        out_shape=out_shape,
        scratch_shapes=[pltpu.VMEM((k, n), BF16)] if cast else [],
        compiler_params=_params("arbitrary"),
        name=name,
    )(*a_parts, w, h, gain.reshape(1, n))
    return tuple(res) if emit_h else res[0]


def _rope_table_kernel(pos_ref, inv_ref, c_ref, sa_ref, sb_ref):
    ang = pos_ref[...].astype(F32) * inv_ref[...]
    c = jnp.cos(ang)
    s = jnp.sin(ang)
    lane = lax.broadcasted_iota(jnp.int32, ang.shape, 1)
    half = MLA_ROPE // 2
    c_ref[...] = jnp.where(lane < MLA_ROPE, c, 0.0)
    sa_ref[...] = jnp.where(lane < half, -s, 0.0)
    sb_ref[...] = jnp.where((lane >= half) & (lane < MLA_ROPE), s, 0.0)


def rope_tables(positions, tm=512):
    t = positions.size
    inv = ROPE_BASE ** (-jnp.arange(0, MLA_ROPE, 2, dtype=F32) / MLA_ROPE)
    inv_row = jnp.concatenate([inv, inv, jnp.zeros((LANES - MLA_ROPE,), F32)]).reshape(1, LANES)
    spec = pl.BlockSpec((tm, LANES), lambda i: (i, 0))
    shape = jax.ShapeDtypeStruct((t, LANES), F32)
    return pl.pallas_call(
        _rope_table_kernel,
        grid=(t // tm,),
        in_specs=[pl.BlockSpec((tm, 1), lambda i: (i, 0)),
                  pl.BlockSpec((1, LANES), lambda i: (0, 0))],
        out_specs=[spec, spec, spec],
        out_shape=[shape, shape, shape],
        compiler_params=_params("parallel"),
        name="rope_tables",
    )(positions.reshape(t, 1), inv_row)


def _rope(x, c, sa, sb):
    half = MLA_ROPE // 2
    return x * c + pltpu.roll(x, LANES - half, 1) * sa + pltpu.roll(x, half, 1) * sb


def _deltanet_kernel(x_ref, ba_ref, cw_ref, alog_ref, dtb_ref, onorm_ref, o_ref, xs_ref, state_ref, *, nch):
    C, D, H = CHUNK, DN_HEAD_DIM, DN_HEADS
    W3 = 3 * DN_WIDTH
    TS = nch * C
    pairs = [(c, h) for c in range(nch) for h in range(H)]

    @pl.when(pl.program_id(1) == 0)
    def _():
        xs_ref[0:SUBLANES, :] = jnp.zeros((SUBLANES, W3), F32)
        state_ref[...] = jnp.zeros_like(state_ref)

    x = x_ref[:, 0:W3]
    xs_ref[SUBLANES:SUBLANES + TS, :] = x
    cw = cw_ref[...]
    y = cw[DN_CONV - 1:DN_CONV, :] * x
    for j in range(DN_CONV - 1):
        shift = DN_CONV - 1 - j
        y = y + cw[j:j + 1, :] * xs_ref[SUBLANES - shift:SUBLANES - shift + TS, :]
    xs_ref[0:SUBLANES, :] = x[TS - SUBLANES:TS, :]
    y = _silu(y)

    ba = ba_ref[...]
    beta_all = jax.nn.sigmoid(ba)
    sp_in = ba + dtb_ref[...]
    softplus = jnp.maximum(sp_in, 0.0) + jnp.log1p(jnp.exp(-jnp.abs(sp_in)))
    g_all = -jnp.exp(alog_ref[...]) * softplus
    row_in_chunk = lax.broadcasted_iota(jnp.int32, (TS, LANES), 0) & (C - 1)
    G = g_all
    step = 1
    while step < C:
        G = G + jnp.where(row_in_chunk >= step, pltpu.roll(G, step, 0), 0.0)
        step *= 2
    Gc = [G[c * C:(c + 1) * C, :] for c in range(nch)]
    GTc = [g.T for g in Gc]
    g_last = [g[C - 1:C, :] for g in Gc]
    e_g = jnp.exp(G)
    e_rem = [jnp.exp(g_last[c] - Gc[c]) for c in range(nch)]
    e_last = [jnp.exp(g) for g in g_last]

    ri = lax.broadcasted_iota(jnp.int32, (C, C), 0)
    ci = lax.broadcasted_iota(jnp.int32, (C, C), 1)
    incl = (ri >= ci)[None]
    strict = (ri > ci)[None]
    eye = jnp.where(ri == ci, 1.0, 0.0).astype(F32)[None]
    onorm = onorm_ref[...]

    def heads(off):
        return jnp.stack([y[c * C:(c + 1) * C, off + h * D:off + (h + 1) * D] for c, h in pairs])

    def head_cols(a, lane0):
        return jnp.stack([a[c * C:(c + 1) * C, lane0 + h:lane0 + h + 1] for c, h in pairs])

    q = heads(0)
    k = heads(DN_WIDTH)
    v = heads(2 * DN_WIDTH)
    q = q * lax.rsqrt(jnp.sum(q * q, axis=-1, keepdims=True) + EPS) * (D ** -0.5)
    k = k * lax.rsqrt(jnp.sum(k * k, axis=-1, keepdims=True) + EPS)
    beta = head_cols(beta_all, 0)
    g_col = head_cols(G, H)
    g_row = jnp.stack([GTc[c][H + h:H + h + 1, :] for c, h in pairs])
    eg_col = head_cols(e_g, H)
    erem_col = jnp.stack([e_rem[c][:, H + h:H + h + 1] for c, h in pairs])
    elast = jnp.stack([e_last[c][:, H + h:H + h + 1] for c, h in pairs])
    decay = jnp.exp(jnp.where(incl, g_col - g_row, -jnp.inf))
    kb = k * beta
    k16 = k.astype(BF16)
    kq = _bdot_nt(jnp.concatenate([kb.astype(BF16), q.astype(BF16)], axis=1), k16)
    lower = jnp.where(strict, kq[:, 0:C] * decay, 0.0)
    attn16 = (kq[:, C:2 * C] * decay).astype(BF16)

    def same_block(bits):
        return (ri >> bits) == (ci >> bits)

    base_bits = 3
    ld = jnp.where((strict[0] & same_block(base_bits))[None], lower, 0.0)
    ld16 = ld.astype(BF16)
    t_inv = eye - ld
    p16 = _bdot(ld16, ld16).astype(BF16)
    r = _bdot(jnp.concatenate([t_inv.astype(BF16), p16], axis=1), p16)
    t_inv = t_inv + r[:, 0:C]
    t_inv = t_inv + _bdot(t_inv.astype(BF16), r[:, C:2 * C].astype(BF16))
    for bits in range(base_bits, C.bit_length() - 1):
        off_diag = (strict[0] & same_block(bits + 1) & jnp.logical_not(same_block(bits)))[None]
        t16 = t_inv.astype(BF16)
        x = _bdot(t16, jnp.where(off_diag, lower, 0.0).astype(BF16))
        t_inv = t_inv - _bdot(x.astype(BF16), t16)

    rhs = jnp.concatenate([v * beta, kb * eg_col], axis=2)
    sol = _bdot(t_inv.astype(BF16), rhs.astype(BF16))
    u = sol[:, :, 0:D]
    wq16 = jnp.concatenate([sol[:, :, D:2 * D].astype(BF16), (q * eg_col).astype(BF16)], axis=1)
    kdec16 = (k * erem_col).astype(BF16)

    s = state_ref[...]
    for c in range(nch):
        sl = slice(c * H, (c + 1) * H)
        ws = _bdot(wq16[sl], s.astype(BF16))
        v_new = u[sl] - ws[:, 0:C]
        vn16 = v_new.astype(BF16)
        o = ws[:, C:2 * C] + _bdot(attn16[sl], vn16)
        s = s * elast[sl] + jnp.einsum("hck,hcv->hkv", kdec16[sl], vn16, preferred_element_type=F32)
        on = _rms(o, onorm[None])
        for h in range(H):
            z = x_ref[c * C:(c + 1) * C, W3 + h * D:W3 + (h + 1) * D]
            o_ref[c * C:(c + 1) * C, h * D:(h + 1) * D] = (on[h] * _silu(z)).astype(o_ref.dtype)
    state_ref[...] = s


def deltanet(proj_main, proj_small, conv_w, a_log, dt_bias, out_norm, batch, seq, nch=DN_CHUNKS_PER_STEP):
    t = batch * seq
    ts = nch * CHUNK
    ns = seq // ts
    pad = jnp.zeros((LANES - 2 * DN_HEADS,), F32)
    alog_row = jnp.concatenate([jnp.zeros((DN_HEADS,), F32), a_log, pad]).reshape(1, LANES)
    dtb_row = jnp.concatenate([jnp.zeros((DN_HEADS,), F32), dt_bias, pad]).reshape(1, LANES)
    w4 = 4 * DN_WIDTH
    return pl.pallas_call(
        functools.partial(_deltanet_kernel, nch=nch),
        grid=(batch, ns),
        in_specs=[pl.BlockSpec((ts, w4), lambda b, c: (b * ns + c, 0)),
                  pl.BlockSpec((ts, LANES), lambda b, c: (b * ns + c, SM_BA // LANES)),
                  pl.BlockSpec((DN_CONV, 3 * DN_WIDTH), lambda b, c: (0, 0)),
                  pl.BlockSpec((1, LANES), lambda b, c: (0, 0)),
                  pl.BlockSpec((1, LANES), lambda b, c: (0, 0)),
                  pl.BlockSpec((1, DN_HEAD_DIM), lambda b, c: (0, 0))],
        out_specs=pl.BlockSpec((ts, DN_WIDTH), lambda b, c: (b * ns + c, 0)),
        out_shape=jax.ShapeDtypeStruct((t, DN_WIDTH), BF16),
        scratch_shapes=[pltpu.VMEM((SUBLANES + ts, 3 * DN_WIDTH), F32),
                        pltpu.VMEM((DN_HEADS, DN_HEAD_DIM, DN_HEAD_DIM), F32)],
        compiler_params=_params("arbitrary", "arbitrary"),
        name="deltanet",
    )(proj_main, proj_small, conv_w, alog_row, dtb_row, out_norm.reshape(1, DN_HEAD_DIM))


def _mla_prep_kernel(x_ref, c_ref, sa_ref, sb_ref, qn_ref, kvn_ref, wq_ref, wkv_ref, qt_ref, k_ref, vt_ref):
    c, sa, sb = c_ref[...], sa_ref[...], sb_ref[...]
    scale = (MLA_NOPE + MLA_ROPE) ** -0.5 * LOG2_E
    q_lat = _rms(x_ref[:, SM_Q:SM_Q + MLA_Q_RANK], qn_ref[...])
    qf = _dot(q_lat.astype(BF16), wq_ref[...]) * scale
    c_kv = _rms(x_ref[:, SM_CKV:SM_CKV + MLA_KV_RANK], kvn_ref[...])
    kvf = _dot(c_kv.astype(BF16), wkv_ref[...])
    k_pe = _rope(x_ref[:, SM_KPE:SM_KPE + LANES], c, sa, sb).astype(k_ref.dtype)
    pad_rows = MLA_VT_ROWS - MLA_V
    ones_row = (lax.broadcasted_iota(jnp.int32, (pad_rows, x_ref.shape[0]), 0) == 0).astype(vt_ref.dtype)
    for h in range(MLA_HEADS):
        o = h * MLA_QK
        q_h = jnp.concatenate([qf[:, o:o + MLA_NOPE], _rope(qf[:, o + MLA_NOPE:o + MLA_QK], c, sa, sb)], axis=1)
        qt_ref[o:o + MLA_QK, :] = q_h.T.astype(qt_ref.dtype)
        k_ref[:, o:o + MLA_NOPE] = kvf[:, o:o + MLA_NOPE].astype(k_ref.dtype)
        k_ref[:, o + MLA_NOPE:o + MLA_QK] = k_pe
        vt_ref[h * MLA_VT_ROWS:h * MLA_VT_ROWS + MLA_V, :] = kvf[:, o + MLA_NOPE:o + MLA_QK].T.astype(vt_ref.dtype)
        vt_ref[h * MLA_VT_ROWS + MLA_V:(h + 1) * MLA_VT_ROWS, :] = ones_row


def mla_prep(proj_small, rope_c, rope_sa, rope_sb, q_norm, kv_norm, wq, wkv, batch, seq, tm):
    t = proj_small.shape[0]
    ns = seq // tm
    hq = MLA_HEADS * MLA_QK
    hv = MLA_HEADS * MLA_VT_ROWS
    row = lambda i: (i, 0)
    fixed = lambda i: (0, 0)
    return pl.pallas_call(
        _mla_prep_kernel,
        grid=(t // tm,),
        in_specs=[pl.BlockSpec((tm, SM_WIDTH), row),
                  pl.BlockSpec((tm, LANES), row), pl.BlockSpec((tm, LANES), row), pl.BlockSpec((tm, LANES), row),
                  pl.BlockSpec((1, MLA_Q_RANK), fixed), pl.BlockSpec((1, MLA_KV_RANK), fixed),
                  pl.BlockSpec((MLA_Q_RANK, hq), fixed), pl.BlockSpec((MLA_KV_RANK, hq), fixed)],
        out_specs=[pl.BlockSpec((None, hq, tm), lambda i: (i // ns, 0, i % ns)),
                   pl.BlockSpec((tm, hq), row),
                   pl.BlockSpec((None, None, hv, tm), lambda i: (i // ns, i % ns, 0, 0))],
        out_shape=[jax.ShapeDtypeStruct((batch, hq, seq), BF16),
                   jax.ShapeDtypeStruct((t, hq), BF16),
                   jax.ShapeDtypeStruct((batch, ns, hv, tm), BF16)],
        compiler_params=_params("parallel"),
        name="mla_prep",
    )(proj_small, rope_c, rope_sa, rope_sb, q_norm.reshape(1, -1), kv_norm.reshape(1, -1), wq, wkv)


def _mla_attn_kernel(qt_ref, k_ref, vt_ref, o_ref, *, tq):
    qi = pl.program_id(1)
    H = MLA_HEADS
    qt = qt_ref[...].reshape(H, MLA_QK, tq)

    def scores(j):
        start = pl.multiple_of(j * tq, tq)
        kk = k_ref[pl.ds(start, tq), :]
        kh = jnp.stack([kk[:, h * MLA_QK:(h + 1) * MLA_QK] for h in range(H)])
        return _bdot(kh, qt)

    def update(carry, s, vtj):
        m, acc = carry
        m_new = jnp.maximum(m, jnp.max(s, axis=1, keepdims=True))
        alpha = jnp.exp2(m - m_new)
        p = jnp.exp2(s - m_new)
        acc = alpha * acc + _bdot(vtj.reshape(H, MLA_VT_ROWS, tq), p.astype(BF16))
        return m_new, acc

    def body(j, carry):
        return update(carry, scores(j), vt_ref[j])

    init = (jnp.full((H, 1, tq), -jnp.inf, F32), jnp.zeros((H, MLA_VT_ROWS, tq), F32))
    carry = lax.fori_loop(0, qi, body, init)
    chunk_bits = CHUNK.bit_length() - 1
    kc = lax.shift_right_logical(lax.broadcasted_iota(jnp.int32, (tq, tq), 0), chunk_bits)
    qc = lax.shift_right_logical(lax.broadcasted_iota(jnp.int32, (tq, tq), 1), chunk_bits)
    s = jnp.where((kc <= qc)[None], scores(qi), -jnp.inf)
    _, acc = update(carry, s, vt_ref[qi])
    o = acc[:, 0:MLA_V] / acc[:, MLA_V:MLA_V + 1]
    for h in range(H):
        o_ref[:, h * MLA_V:(h + 1) * MLA_V] = o[h].T.astype(o_ref.dtype)


def mla_attention(qt, k, vt, batch, seq, tq):
    t = batch * seq
    nq = seq // tq
    hq = MLA_HEADS * MLA_QK
    hv = MLA_HEADS * MLA_V
    hvt = MLA_HEADS * MLA_VT_ROWS
    return pl.pallas_call(
        functools.partial(_mla_attn_kernel, tq=tq),
        grid=(batch, nq),
        in_specs=[pl.BlockSpec((None, hq, tq), lambda b, i: (b, 0, i)),
                  pl.BlockSpec((seq, hq), lambda b, i: (b, 0)),
                  pl.BlockSpec((None, nq, hvt, tq), lambda b, i: (b, 0, 0, 0))],
        out_specs=pl.BlockSpec((tq, hv), lambda b, i: (b * nq + i, 0)),
        out_shape=jax.ShapeDtypeStruct((t, hv), BF16),
        compiler_params=_params("parallel", "parallel"),
        name="mla_attention",
    )(qt, k, vt)


def _xattn_kernel(q_ref, k_ref, v_ref, o_ref):
    scale = XA_HEAD_DIM ** -0.5
    for h in range(XA_HEADS):
        sl = slice(h * XA_HEAD_DIM, (h + 1) * XA_HEAD_DIM)
        s = _dot_nt(q_ref[:, sl], k_ref[:, sl]) * scale
        p = jnp.exp(s - jnp.max(s, axis=-1, keepdims=True))
        l = jnp.sum(p, axis=-1, keepdims=True)
        o_ref[:, sl] = (_dot(p.astype(BF16), v_ref[:, sl]) / l).astype(o_ref.dtype)


def cross_attention(q, k, v, batch, seq, n_mem, tq=512):
    t, d = q.shape
    nq = seq // tq
    return pl.pallas_call(
        _xattn_kernel,
        grid=(batch, nq),
        in_specs=[pl.BlockSpec((tq, d), lambda b, i: (b * nq + i, 0)),
                  pl.BlockSpec((n_mem, d), lambda b, i: (b, 0)),
                  pl.BlockSpec((n_mem, d), lambda b, i: (b, 0))],
        out_specs=pl.BlockSpec((tq, d), lambda b, i: (b * nq + i, 0)),
        out_shape=jax.ShapeDtypeStruct((t, d), BF16),
        compiler_params=_params("parallel", "parallel"),
        name="cross_attention",
    )(q, k, v)


def _ffn_up_kernel(a_ref, wg_ref, wu_ref, cg_ref, cu_ref, bg_ref, bu_ref, o_ref, eg_ref, eu_ref, wg16_ref, wu16_ref,
                   *, tiles_per_seq):
    i = pl.program_id(1)
    tm = a_ref.shape[0]

    @pl.when(i == 0)
    def _():
        wg16_ref[...] = wg_ref[...].astype(BF16)
        wu16_ref[...] = wu_ref[...].astype(BF16)

    @pl.when(i % tiles_per_seq == 0)
    def _():
        eg_ref[0:SUBLANES, :] = jnp.zeros((SUBLANES, eg_ref.shape[1]), F32)
        eu_ref[0:SUBLANES, :] = jnp.zeros((SUBLANES, eu_ref.shape[1]), F32)

    a = a_ref[...]

    def conv(w_ref, cw_ref, b_ref, e_ref):
        acc = _dot(a, w_ref[...])
        e_ref[SUBLANES:SUBLANES + tm, :] = acc
        cw = cw_ref[...]
        y = cw[FFN_CONV - 1:FFN_CONV, :] * acc + b_ref[...]
        for j in range(FFN_CONV - 1):
            shift = FFN_CONV - 1 - j
            y = y + cw[j:j + 1, :] * e_ref[SUBLANES - shift:SUBLANES - shift + tm, :]
        e_ref[0:SUBLANES, :] = acc[tm - SUBLANES:tm, :]
        return y

    gate = conv(wg16_ref, cg_ref, bg_ref, eg_ref)
    up = conv(wu16_ref, cu_ref, bu_ref, eu_ref)
    o_ref[...] = (_silu(gate) * up).astype(o_ref.dtype)


def ffn_up(a, w_up, layer, conv_w, conv_b, seq, tm=1024, tn=512):
    m, k = a.shape
    nb = D_FF // tn
    return pl.pallas_call(
        functools.partial(_ffn_up_kernel, tiles_per_seq=seq // tm),
        grid=(nb, m // tm),
        in_specs=[pl.BlockSpec((tm, k), lambda j, i: (i, 0)),
                  _weight_spec(w_up, layer, (k, tn), lambda j, i: (0, j)),
                  _weight_spec(w_up, layer, (k, tn), lambda j, i: (0, j + nb)),
                  pl.BlockSpec((FFN_CONV, tn), lambda j, i: (0, j)),
                  pl.BlockSpec((FFN_CONV, tn), lambda j, i: (0, j + nb)),
                  pl.BlockSpec((1, tn), lambda j, i: (0, j)),
                  pl.BlockSpec((1, tn), lambda j, i: (0, j + nb))],
        out_specs=pl.BlockSpec((tm, tn), lambda j, i: (i, j)),
        out_shape=jax.ShapeDtypeStruct((m, D_FF), BF16),
        scratch_shapes=[pltpu.VMEM((SUBLANES + tm, tn), F32), pltpu.VMEM((SUBLANES + tm, tn), F32),
                        pltpu.VMEM((k, tn), BF16), pltpu.VMEM((k, tn), BF16)],
        compiler_params=_params("parallel", "arbitrary"),
        name="ffn_up",
    )(a, w_up, w_up, conv_w, conv_w, conv_b.reshape(1, -1), conv_b.reshape(1, -1))


def _prep_in_small(w_in_t, layer):
    d = w_in_t.shape[2]
    base = 4 * DN_WIDTH
    rows = lambda a, n: lax.slice(w_in_t, (layer, a, 0), (layer + 1, a + n, d)).reshape(n, d)
    wb = rows(base, DN_HEADS)
    wa = rows(base + DN_HEADS, DN_HEADS)
    mq0 = base + 2 * DN_HEADS
    wq = rows(mq0, MLA_Q_RANK)
    wckv = rows(mq0 + MLA_Q_RANK, MLA_KV_RANK)
    wkpe = rows(mq0 + MLA_Q_RANK + MLA_KV_RANK, MLA_ROPE)
    z = lambda n: jnp.zeros((n, d), w_in_t.dtype)
    return jnp.concatenate([wq, wckv, wkpe, z(LANES - MLA_ROPE), wb, wa, z(LANES - 2 * DN_HEADS)], axis=0)


def _prep_wqb(w):
    r = w.shape[0]
    w3 = w.reshape(r, MLA_HEADS, MLA_NOPE + MLA_ROPE)
    w3 = jnp.concatenate([w3, jnp.zeros((r, MLA_HEADS, LANES - MLA_ROPE), w.dtype)], axis=2)
    return w3.reshape(r, MLA_HEADS * MLA_QK)


def kernel(x, mem, positions, norm_mix, w_in, dn_conv, dn_a_log, dn_dt_bias, dn_out_norm, mla_q_norm,
           mla_w_qb, mla_kv_norm, mla_w_kvb, w_out, mem_norm, norm_xattn, xa_wq, xa_wk, xa_wv, xa_wo,
           norm_ffn, ffn_w_up, ffn_conv, ffn_conv_bias, ffn_w_down, norm_final):
    batch, seq, d = x.shape
    n_mem = mem.shape[1]
    t = batch * seq
    depth = w_in.shape[0]

    rope_c, rope_sa, rope_sb = rope_tables(positions)
    mem_n = rmsnorm_rows(mem.reshape(batch * n_mem, d), mem_norm, BF16, tm=256)
    h = x.reshape(t, d)
    u = rmsnorm_rows(h, norm_mix[0], BF16)

    w_in_t = jnp.swapaxes(w_in, 1, 2)

    for l in range(depth):
        proj_main = matmul(u, w_in_t, F32, tm=1024, tn=1024, name="in_proj_main", layer=l, n_cols=4 * DN_WIDTH,
                           w_transposed=True)
        proj_small = matmul(u, _prep_in_small(w_in_t, l), F32, tm=1024, tn=SM_WIDTH, name="in_proj_small",
                            w_transposed=True)

        o_dn = deltanet(proj_main, proj_small, dn_conv[l], dn_a_log[l], dn_dt_bias[l], dn_out_norm[l], batch, seq)

        qt, k, vt = mla_prep(proj_small, rope_c, rope_sa, rope_sb, mla_q_norm[l], mla_kv_norm[l],
                             _prep_wqb(mla_w_qb[l]).astype(BF16), mla_w_kvb[l].astype(BF16),
                             batch, seq, tm=MLA_TILE)
        o_mla = mla_attention(qt, k, vt, batch, seq, tq=MLA_TILE)

        h, u = matmul_residual_norm([o_dn, o_mla], w_out, h, norm_xattn[l], BF16, tm=256, name="out_proj", layer=l)

        xq, w_down16 = matmul(u, xa_wq, BF16, tm=1024, tn=1024, name="xa_q", layer=l, side_cast=(ffn_w_down, l))
        xk = matmul(mem_n, xa_wk, BF16, tm=1024, tn=1024, name="xa_k", layer=l)
        xv = matmul(mem_n, xa_wv, BF16, tm=1024, tn=1024, name="xa_v", layer=l)
        o_xa = cross_attention(xq, xk, xv, batch, seq, n_mem)
        h, u = matmul_residual_norm([o_xa], xa_wo, h, norm_ffn[l], BF16, tm=256, name="xa_o", layer=l)

        hid = ffn_up(u, ffn_w_up, l, ffn_conv[l], ffn_conv_bias[l], seq)
        if l + 1 < depth:
            h, u = matmul_residual_norm([hid], w_down16, h, norm_mix[l + 1], BF16, tm=256, name="ffn_down")
        else:
            out = matmul_residual_norm([hid], w_down16, h, norm_final, x.dtype, tm=256, emit_h=False,
                                       name="ffn_down_final")
    return out.reshape(batch, seq, d)
```

```python
import functools

import numpy as np
import jax
import jax.numpy as jnp
from jax import lax
from jax.experimental import pallas as pl
from jax.experimental.pallas import tpu as pltpu

F32 = jnp.float32
BF16 = jnp.bfloat16

D_MODEL = 2048
DEPTH = 4
CHUNK = 64
DN_HEADS = 8
DN_HEAD_DIM = 128
DN_WIDTH = DN_HEADS * DN_HEAD_DIM
DN_CONV = 4
MLA_HEADS = 8
MLA_NOPE = 128
MLA_ROPE = 64
MLA_V = 128
MLA_Q_RANK = 512
MLA_KV_RANK = 256
ROPE_BASE = 10000.0
XA_HEADS = 4
XA_HEAD_DIM = D_MODEL // XA_HEADS
D_FF = 5632
FFN_CONV = 3
EPS = 1e-6
LOG2_E = float(np.log2(np.e))

LANES = 128
SUBLANES = 8
BF16_SUBLANES = 16
MXU_COLS = 256
VMEM_LIMIT = 56 * 1024 * 1024

SM_Q = 0
SM_CKV = MLA_Q_RANK
SM_KPE = SM_CKV + MLA_KV_RANK
SM_BA = SM_KPE + LANES
SM_WIDTH = SM_BA + LANES
MLA_QK = MLA_NOPE + LANES
MLA_VT_ROWS = MLA_V + BF16_SUBLANES
MLA_TILE = 512
DN_CHUNKS_PER_STEP = 4


def _params(*sem):
    return pltpu.CompilerParams(dimension_semantics=sem, vmem_limit_bytes=VMEM_LIMIT)


def _dot(a, b):
    return jnp.dot(a, b, preferred_element_type=F32)


def _dot_nt(a, b):
    return lax.dot_general(a, b, (((1,), (1,)), ((), ())), preferred_element_type=F32)


def _bdot(a, b):
    return jnp.einsum("hmk,hkn->hmn", a, b, preferred_element_type=F32)


def _bdot_nt(a, b):
    return jnp.einsum("hmk,hnk->hmn", a, b, preferred_element_type=F32)


def _rms(x, gain):
    return x * lax.rsqrt(jnp.mean(x * x, axis=-1, keepdims=True) + EPS) * gain


def _silu(x):
    return x * jax.nn.sigmoid(x)


def _rmsnorm_kernel(x_ref, g_ref, o_ref):
    o_ref[...] = _rms(x_ref[...].astype(F32), g_ref[...]).astype(o_ref.dtype)


def rmsnorm_rows(x, gain, out_dtype, tm=512):
    m, d = x.shape
    return pl.pallas_call(
        _rmsnorm_kernel,
        grid=(m // tm,),
        in_specs=[pl.BlockSpec((tm, d), lambda i: (i, 0)),
                  pl.BlockSpec((1, d), lambda i: (0, 0))],
        out_specs=pl.BlockSpec((tm, d), lambda i: (i, 0)),
        out_shape=jax.ShapeDtypeStruct((m, d), out_dtype),
        compiler_params=_params("parallel"),
        name="rmsnorm_rows",
    )(x, gain.reshape(1, d))


def _weight_spec(w, layer, block, index_fn, **kw):
    if layer is None:
        return pl.BlockSpec(block, index_fn, **kw)
    return pl.BlockSpec((None,) + block, lambda *g: (layer,) + index_fn(*g), **kw)


def _mm_kernel(a_ref, w_ref, *rest, w_transposed, side_cast):
    if side_cast:
        src_ref, o_ref, dst_ref = rest[:3]
        scratch = rest[3:]
        dst_ref[...] = src_ref[...].astype(dst_ref.dtype)
    else:
        o_ref = rest[0]
        scratch = rest[1:]
    if scratch:
        w16_ref, = scratch

        @pl.when(pl.program_id(1) == 0)
        def _():
            if w_transposed:
                for c in range(0, w_ref.shape[0], MXU_COLS):
                    w16_ref[:, c:c + MXU_COLS] = w_ref[c:c + MXU_COLS, :].T.astype(BF16)
            else:
                w16_ref[...] = w_ref[...].astype(BF16)

        w = w16_ref[...]
    else:
        w = w_ref[...]
    o_ref[...] = _dot(a_ref[...], w).astype(o_ref.dtype)


def matmul(a, w, out_dtype, tm, tn, name, layer=None, n_cols=None, w_transposed=False, side_cast=None):
    m, k = a.shape
    cast = w.dtype != BF16
    assert cast or not w_transposed
    n = w.shape[-2 if w_transposed else -1] if n_cols is None else n_cols
    nm = m // tm
    if w_transposed:
        w_spec = _weight_spec(w, layer, (tn, k), lambda j, i: (j, 0))
    else:
        w_spec = _weight_spec(w, layer, (k, tn), lambda j, i: (0, j))
    in_specs = [pl.BlockSpec((tm, k), lambda j, i: (i, 0)), w_spec]
    out_specs = [pl.BlockSpec((tm, tn), lambda j, i: (i, j))]
    out_shape = [jax.ShapeDtypeStruct((m, n), out_dtype)]
    operands = [a, w]
    if side_cast is not None:
        src, src_layer = side_cast
        rows, cols = src.shape[1] // ((n // tn) * nm), src.shape[2]
        in_specs.append(pl.BlockSpec((None, rows, cols), lambda j, i: (src_layer, j * nm + i, 0)))
        out_specs.append(pl.BlockSpec((rows, cols), lambda j, i: (j * nm + i, 0)))
        out_shape.append(jax.ShapeDtypeStruct(src.shape[1:], BF16))
        operands.append(src)
    res = pl.pallas_call(
        functools.partial(_mm_kernel, w_transposed=w_transposed, side_cast=side_cast is not None),
        grid=(n // tn, nm),
        in_specs=in_specs,
        out_specs=out_specs,
        out_shape=out_shape,
        scratch_shapes=[pltpu.VMEM((k, tn), BF16)] if cast else [],
        compiler_params=_params("parallel", "arbitrary"),
        name=name,
    )(*operands)
    return res[0] if side_cast is None else tuple(res)


def _mm_res_norm_kernel(*refs, n_parts, emit_h, cast):
    a_refs = refs[:n_parts]
    w_ref, h_ref, g_ref = refs[n_parts:n_parts + 3]
    outs = refs[n_parts + 3:]
    ho_ref = outs[0] if emit_h else None
    uo_ref = outs[1] if emit_h else outs[0]
    if cast:
        w16_ref = outs[-1]

        @pl.when(pl.program_id(0) == 0)
        def _():
            w16_ref[...] = w_ref[...].astype(BF16)

        w_ref = w16_ref
    off = 0
    acc = None
    for a_ref in a_refs:
        kp = a_ref.shape[1]
        part = _dot(a_ref[...], w_ref[off:off + kp, :])
        acc = part if acc is None else acc + part
        off += kp
    h = h_ref[...] + acc
    if emit_h:
        ho_ref[...] = h
    uo_ref[...] = _rms(h, g_ref[...]).astype(uo_ref.dtype)


def matmul_residual_norm(a_parts, w, h, gain, u_dtype, tm, name, layer=None, emit_h=True):
    m, n = h.shape
    k = w.shape[-2]
    n_parts = len(a_parts)
    cast = w.dtype != BF16
    row = lambda i: (i, 0)
    in_specs = [pl.BlockSpec((tm, a.shape[1]), row) for a in a_parts] + [
        _weight_spec(w, layer, (k, n), lambda i: (0, 0), pipeline_mode=pl.Buffered(1)),
        pl.BlockSpec((tm, n), row),
        pl.BlockSpec((1, n), lambda i: (0, 0)),
    ]
    out_specs = [pl.BlockSpec((tm, n), row)]
    out_shape = [jax.ShapeDtypeStruct((m, n), u_dtype)]
    if emit_h:
        out_specs = [pl.BlockSpec((tm, n), row)] + out_specs
        out_shape = [jax.ShapeDtypeStruct((m, n), F32)] + out_shape
    res = pl.pallas_call(
        functools.partial(_mm_res_norm_kernel, n_parts=n_parts, emit_h=emit_h, cast=cast),
        grid=(m // tm,),
        in_specs=in_specs,
        out_specs=out_specs,
        out_shape=out_shape,
        scratch_shapes=[pltpu.VMEM((k, n), BF16)] if cast else [],
        compiler_params=_params("arbitrary"),
        name=name,
    )(*a_parts, w, h, gain.reshape(1, n))
    return tuple(res) if emit_h else res[0]


def _rope_table_kernel(pos_ref, inv_ref, c_ref, sa_ref, sb_ref):
    ang = pos_ref[...].astype(F32) * inv_ref[...]
    c = jnp.cos(ang)
    s = jnp.sin(ang)
    lane = lax.broadcasted_iota(jnp.int32, ang.shape, 1)
    half = MLA_ROPE // 2
    c_ref[...] = jnp.where(lane < MLA_ROPE, c, 0.0)
    sa_ref[...] = jnp.where(lane < half, -s, 0.0)
    sb_ref[...] = jnp.where((lane >= half) & (lane < MLA_ROPE), s, 0.0)


def rope_tables(positions, tm=512):
    t = positions.size
    inv = ROPE_BASE ** (-jnp.arange(0, MLA_ROPE, 2, dtype=F32) / MLA_ROPE)
    inv_row = jnp.concatenate([inv, inv, jnp.zeros((LANES - MLA_ROPE,), F32)]).reshape(1, LANES)
    spec = pl.BlockSpec((tm, LANES), lambda i: (i, 0))
    shape = jax.ShapeDtypeStruct((t, LANES), F32)
    return pl.pallas_call(
        _rope_table_kernel,
        grid=(t // tm,),
        in_specs=[pl.BlockSpec((tm, 1), lambda i: (i, 0)),
                  pl.BlockSpec((1, LANES), lambda i: (0, 0))],
        out_specs=[spec, spec, spec],
        out_shape=[shape, shape, shape],
        compiler_params=_params("parallel"),
        name="rope_tables",
    )(positions.reshape(t, 1), inv_row)


def _rope(x, c, sa, sb):
    half = MLA_ROPE // 2
    return x * c + pltpu.roll(x, LANES - half, 1) * sa + pltpu.roll(x, half, 1) * sb


def _deltanet_kernel(x_ref, ba_ref, cw_ref, alog_ref, dtb_ref, onorm_ref, o_ref, xs_ref, state_ref, *, nch):
    C, D, H = CHUNK, DN_HEAD_DIM, DN_HEADS
    W3 = 3 * DN_WIDTH
    TS = nch * C
    pairs = [(c, h) for c in range(nch) for h in range(H)]

    @pl.when(pl.program_id(1) == 0)
    def _():
        xs_ref[0:SUBLANES, :] = jnp.zeros((SUBLANES, W3), F32)
        state_ref[...] = jnp.zeros_like(state_ref)

    x = x_ref[:, 0:W3]
    xs_ref[SUBLANES:SUBLANES + TS, :] = x
    cw = cw_ref[...]
    y = cw[DN_CONV - 1:DN_CONV, :] * x
    for j in range(DN_CONV - 1):
        shift = DN_CONV - 1 - j
        y = y + cw[j:j + 1, :] * xs_ref[SUBLANES - shift:SUBLANES - shift + TS, :]
    xs_ref[0:SUBLANES, :] = x[TS - SUBLANES:TS, :]
    y = _silu(y)

    ba = ba_ref[...]
    beta_all = jax.nn.sigmoid(ba)
    sp_in = ba + dtb_ref[...]
    softplus = jnp.maximum(sp_in, 0.0) + jnp.log1p(jnp.exp(-jnp.abs(sp_in)))
    g_all = -jnp.exp(alog_ref[...]) * softplus
    row_in_chunk = lax.broadcasted_iota(jnp.int32, (TS, LANES), 0) & (C - 1)
    G = g_all
    step = 1
    while step < C:
        G = G + jnp.where(row_in_chunk >= step, pltpu.roll(G, step, 0), 0.0)
        step *= 2
    Gc = [G[c * C:(c + 1) * C, :] for c in range(nch)]
    GTc = [g.T for g in Gc]
    g_last = [g[C - 1:C, :] for g in Gc]
    e_g = jnp.exp(G)
    e_rem = [jnp.exp(g_last[c] - Gc[c]) for c in range(nch)]
    e_last = [jnp.exp(g) for g in g_last]

    ri = lax.broadcasted_iota(jnp.int32, (C, C), 0)
    ci = lax.broadcasted_iota(jnp.int32, (C, C), 1)
    incl = (ri >= ci)[None]
    strict = (ri > ci)[None]
    eye = jnp.where(ri == ci, 1.0, 0.0).astype(F32)[None]
    onorm = onorm_ref[...]

    def heads(off):
        return jnp.stack([y[c * C:(c + 1) * C, off + h * D:off + (h + 1) * D] for c, h in pairs])

    def head_cols(a, lane0):
        return jnp.stack([a[c * C:(c + 1) * C, lane0 + h:lane0 + h + 1] for c, h in pairs])

    q = heads(0)
    k = heads(DN_WIDTH)
    v = heads(2 * DN_WIDTH)
    q = q * lax.rsqrt(jnp.sum(q * q, axis=-1, keepdims=True) + EPS) * (D ** -0.5)
    k = k * lax.rsqrt(jnp.sum(k * k, axis=-1, keepdims=True) + EPS)
    beta = head_cols(beta_all, 0)
    g_col = head_cols(G, H)
    g_row = jnp.stack([GTc[c][H + h:H + h + 1, :] for c, h in pairs])
    eg_col = head_cols(e_g, H)
    erem_col = jnp.stack([e_rem[c][:, H + h:H + h + 1] for c, h in pairs])
    elast = jnp.stack([e_last[c][:, H + h:H + h + 1] for c, h in pairs])
    decay = jnp.exp(jnp.where(incl, g_col - g_row, -jnp.inf))
    kb = k * beta
    k16 = k.astype(BF16)
    kq = _bdot_nt(jnp.concatenate([kb.astype(BF16), q.astype(BF16)], axis=1), k16)
    lower = jnp.where(strict, kq[:, 0:C] * decay, 0.0)
    attn16 = (kq[:, C:2 * C] * decay).astype(BF16)

    def same_block(bits):
        return (ri >> bits) == (ci >> bits)

    base_bits = SUBLANES.bit_length() - 1
    ld =jnp.where((strict[0] & same_block(base_bits))[None], lower, 0.0)
    ld16 = ld.astype(BF16)
    t_inv = eye - ld
    p16 = _bdot(ld16, ld16).astype(BF16)
    r = _bdot(jnp.concatenate([t_inv.astype(BF16), p16], axis=1), p16)
    t_inv = t_inv + r[:, 0:C]
    t_inv = t_inv + _bdot(t_inv.astype(BF16), r[:, C:2 * C].astype(BF16))
    for bits in range(base_bits, C.bit_length() - 1):
        off_diag = (strict[0] & same_block(bits + 1) & jnp.logical_not(same_block(bits)))[None]
        t16 = t_inv.astype(BF16)
        x = _bdot(t16, jnp.where(off_diag, lower, 0.0).astype(BF16))
        t_inv = t_inv - _bdot(x.astype(BF16), t16)

    rhs = jnp.concatenate([v * beta, kb * eg_col], axis=2)
    sol = _bdot(t_inv.astype(BF16), rhs.astype(BF16))
    u = sol[:, :, 0:D]
    wq16 = jnp.concatenate([sol[:, :, D:2 * D].astype(BF16), (q * eg_col).astype(BF16)], axis=1)
    kdec16 = (k * erem_col).astype(BF16)

    s = state_ref[...]
    for c in range(nch):
        sl = slice(c * H, (c + 1) * H)
        ws = _bdot(wq16[sl], s.astype(BF16))
        v_new = u[sl] - ws[:, 0:C]
        vn16 = v_new.astype(BF16)
        o = ws[:, C:2 * C] + _bdot(attn16[sl], vn16)
        s = s * elast[sl] + jnp.einsum("hck,hcv->hkv", kdec16[sl], vn16, preferred_element_type=F32)
        on = _rms(o, onorm[None])
        for h in range(H):
            z = x_ref[c * C:(c + 1) * C, W3 + h * D:W3 + (h + 1) * D]
            o_ref[c * C:(c + 1) * C, h * D:(h + 1) * D] = (on[h] * _silu(z)).astype(o_ref.dtype)
    state_ref[...] = s


def deltanet(proj_main, proj_small, conv_w, a_log, dt_bias, out_norm, batch, seq, nch=DN_CHUNKS_PER_STEP):
    t = batch * seq
    ts = nch * CHUNK
    ns = seq // ts
    pad = jnp.zeros((LANES - 2 * DN_HEADS,), F32)
    alog_row = jnp.concatenate([jnp.zeros((DN_HEADS,), F32), a_log, pad]).reshape(1, LANES)
    dtb_row = jnp.concatenate([jnp.zeros((DN_HEADS,), F32), dt_bias, pad]).reshape(1, LANES)
    w4 = 4 * DN_WIDTH
    return pl.pallas_call(
        functools.partial(_deltanet_kernel, nch=nch),
        grid=(batch, ns),
        in_specs=[pl.BlockSpec((ts, w4), lambda b, c: (b * ns + c, 0)),
                  pl.BlockSpec((ts, LANES), lambda b, c: (b * ns + c, SM_BA // LANES)),
                  pl.BlockSpec((DN_CONV, 3 * DN_WIDTH), lambda b, c: (0, 0)),
                  pl.BlockSpec((1, LANES), lambda b, c: (0, 0)),
                  pl.BlockSpec((1, LANES), lambda b, c: (0, 0)),
                  pl.BlockSpec((1, DN_HEAD_DIM), lambda b, c: (0, 0))],
        out_specs=pl.BlockSpec((ts, DN_WIDTH), lambda b, c: (b * ns + c, 0)),
        out_shape=jax.ShapeDtypeStruct((t, DN_WIDTH), BF16),
        scratch_shapes=[pltpu.VMEM((SUBLANES + ts, 3 * DN_WIDTH), F32),
                        pltpu.VMEM((DN_HEADS, DN_HEAD_DIM, DN_HEAD_DIM), F32)],
        compiler_params=_params("arbitrary", "arbitrary"),
        name="deltanet",
    )(proj_main, proj_small, conv_w, alog_row, dtb_row, out_norm.reshape(1, DN_HEAD_DIM))


def _mla_prep_kernel(x_ref, c_ref, sa_ref, sb_ref, qn_ref, kvn_ref, wq_ref, wkv_ref, qt_ref, k_ref, vt_ref):
    c, sa, sb = c_ref[...], sa_ref[...], sb_ref[...]
    scale = (MLA_NOPE + MLA_ROPE) ** -0.5 * LOG2_E
    q_lat = _rms(x_ref[:, SM_Q:SM_Q + MLA_Q_RANK], qn_ref[...])
    qf = _dot(q_lat.astype(BF16), wq_ref[...]) * scale
    c_kv = _rms(x_ref[:, SM_CKV:SM_CKV + MLA_KV_RANK], kvn_ref[...])
    kvf = _dot(c_kv.astype(BF16), wkv_ref[...])
    k_pe = _rope(x_ref[:, SM_KPE:SM_KPE + LANES], c, sa, sb).astype(k_ref.dtype)
    pad_rows = MLA_VT_ROWS - MLA_V
    ones_row = (lax.broadcasted_iota(jnp.int32, (pad_rows, x_ref.shape[0]), 0) == 0).astype(vt_ref.dtype)
    for h in range(MLA_HEADS):
        o = h * MLA_QK
        q_h = jnp.concatenate([qf[:, o:o + MLA_NOPE], _rope(qf[:, o + MLA_NOPE:o + MLA_QK], c, sa, sb)], axis=1)
        qt_ref[o:o + MLA_QK, :] = q_h.T.astype(qt_ref.dtype)
        k_ref[:, o:o + MLA_NOPE] = kvf[:, o:o + MLA_NOPE].astype(k_ref.dtype)
        k_ref[:, o + MLA_NOPE:o + MLA_QK] = k_pe
        vt_ref[h * MLA_VT_ROWS:h * MLA_VT_ROWS + MLA_V, :] = kvf[:, o + MLA_NOPE:o + MLA_QK].T.astype(vt_ref.dtype)
        vt_ref[h * MLA_VT_ROWS + MLA_V:(h + 1) * MLA_VT_ROWS, :] = ones_row


def mla_prep(proj_small, rope_c, rope_sa, rope_sb, q_norm, kv_norm, wq, wkv, batch, seq, tm):
    t = proj_small.shape[0]
    ns = seq // tm
    hq = MLA_HEADS * MLA_QK
    hv = MLA_HEADS * MLA_VT_ROWS
    row = lambda i: (i, 0)
    fixed = lambda i: (0, 0)
    return pl.pallas_call(
        _mla_prep_kernel,
        grid=(t // tm,),
        in_specs=[pl.BlockSpec((tm, SM_WIDTH), row),
                  pl.BlockSpec((tm, LANES), row), pl.BlockSpec((tm, LANES), row), pl.BlockSpec((tm, LANES), row),
                  pl.BlockSpec((1, MLA_Q_RANK), fixed), pl.BlockSpec((1, MLA_KV_RANK), fixed),
                  pl.BlockSpec((MLA_Q_RANK, hq), fixed), pl.BlockSpec((MLA_KV_RANK, hq), fixed)],
        out_specs=[pl.BlockSpec((None, hq, tm), lambda i: (i // ns, 0, i % ns)),
                   pl.BlockSpec((tm, hq), row),
                   pl.BlockSpec((None, None, hv, tm), lambda i: (i // ns, i % ns, 0, 0))],
        out_shape=[jax.ShapeDtypeStruct((batch, hq, seq), BF16),
                   jax.ShapeDtypeStruct((t, hq), BF16),
                   jax.ShapeDtypeStruct((batch, ns, hv, tm), BF16)],
        compiler_params=_params("parallel"),
        name="mla_prep",
    )(proj_small, rope_c, rope_sa, rope_sb, q_norm.reshape(1, -1), kv_norm.reshape(1, -1), wq, wkv)


def _mla_attn_kernel(qt_ref, k_ref, vt_ref, o_ref, *, tq):
    qi = pl.program_id(1)
    H = MLA_HEADS
    qt = qt_ref[...].reshape(H, MLA_QK, tq)

    def scores(j):
        start = pl.multiple_of(j * tq, tq)
        kk = k_ref[pl.ds(start, tq), :]
        kh = jnp.stack([kk[:, h * MLA_QK:(h + 1) * MLA_QK] for h in range(H)])
        return _bdot(kh, qt)

    def update(carry, s, vtj):
        m, acc = carry
        m_new = jnp.maximum(m, jnp.max(s, axis=1, keepdims=True))
        alpha = jnp.exp2(m - m_new)
        p = jnp.exp2(s - m_new)
        acc = alpha * acc + _bdot(vtj.reshape(H, MLA_VT_ROWS, tq), p.astype(BF16))
        return m_new, acc

    def body(j, carry):
        return update(carry, scores(j), vt_ref[j])

    init = (jnp.full((H, 1, tq), -jnp.inf, F32), jnp.zeros((H, MLA_VT_ROWS, tq), F32))
    carry = lax.fori_loop(0, qi, body, init)
    chunk_bits = CHUNK.bit_length() - 1
    kc = lax.shift_right_logical(lax.broadcasted_iota(jnp.int32, (tq, tq), 0), chunk_bits)
    qc = lax.shift_right_logical(lax.broadcasted_iota(jnp.int32, (tq, tq), 1), chunk_bits)
    s = jnp.where((kc <= qc)[None], scores(qi), -jnp.inf)
    _, acc = update(carry, s, vt_ref[qi])
    o = acc[:, 0:MLA_V] / acc[:, MLA_V:MLA_V + 1]
    for h in range(H):
        o_ref[:, h * MLA_V:(h + 1) * MLA_V] = o[h].T.astype(o_ref.dtype)


def mla_attention(qt, k, vt, batch, seq, tq):
    t = batch * seq
    nq = seq // tq
    hq = MLA_HEADS * MLA_QK
    hv = MLA_HEADS * MLA_V
    hvt = MLA_HEADS * MLA_VT_ROWS
    return pl.pallas_call(
        functools.partial(_mla_attn_kernel, tq=tq),
        grid=(batch, nq),
        in_specs=[pl.BlockSpec((None, hq, tq), lambda b, i: (b, 0, i)),
                  pl.BlockSpec((seq, hq), lambda b, i: (b, 0)),
                  pl.BlockSpec((None, nq, hvt, tq), lambda b, i: (b, 0, 0, 0))],
        out_specs=pl.BlockSpec((tq, hv), lambda b, i: (b * nq + i, 0)),
        out_shape=jax.ShapeDtypeStruct((t, hv), BF16),
        compiler_params=_params("parallel", "parallel"),
        name="mla_attention",
    )(qt, k, vt)


def _xattn_kernel(q_ref, k_ref, v_ref, o_ref):
    scale = XA_HEAD_DIM ** -0.5
    for h in range(XA_HEADS):
        sl = slice(h * XA_HEAD_DIM, (h + 1) * XA_HEAD_DIM)
        s = _dot_nt(q_ref[:, sl], k_ref[:, sl]) * scale
        p = jnp.exp(s - jnp.max(s, axis=-1, keepdims=True))
        l = jnp.sum(p, axis=-1, keepdims=True)
        o_ref[:, sl] = (_dot(p.astype(BF16), v_ref[:, sl]) / l).astype(o_ref.dtype)


def cross_attention(q, k, v, batch, seq, n_mem, tq=512):
    t, d = q.shape
    nq = seq // tq
    return pl.pallas_call(
        _xattn_kernel,
        grid=(batch, nq),
        in_specs=[pl.BlockSpec((tq, d), lambda b, i: (b * nq + i, 0)),
                  pl.BlockSpec((n_mem, d), lambda b, i: (b, 0)),
                  pl.BlockSpec((n_mem, d), lambda b, i: (b, 0))],
        out_specs=pl.BlockSpec((tq, d), lambda b, i: (b * nq + i, 0)),
        out_shape=jax.ShapeDtypeStruct((t, d), BF16),
        compiler_params=_params("parallel", "parallel"),
        name="cross_attention",
    )(q, k, v)


def _ffn_up_kernel(a_ref, wg_ref, wu_ref, cg_ref, cu_ref, bg_ref, bu_ref, o_ref, eg_ref, eu_ref, wg16_ref, wu16_ref,
                   *, tiles_per_seq):
    i = pl.program_id(1)
    tm = a_ref.shape[0]

    @pl.when(i == 0)
    def _():
        wg16_ref[...] = wg_ref[...].astype(BF16)
        wu16_ref[...] = wu_ref[...].astype(BF16)

    @pl.when(i % tiles_per_seq == 0)
    def _():
        eg_ref[0:SUBLANES, :] = jnp.zeros((SUBLANES, eg_ref.shape[1]), F32)
        eu_ref[0:SUBLANES, :] = jnp.zeros((SUBLANES, eu_ref.shape[1]), F32)

    a = a_ref[...]

    def conv(w_ref, cw_ref, b_ref, e_ref):
        acc = _dot(a, w_ref[...])
        e_ref[SUBLANES:SUBLANES + tm, :] = acc
        cw = cw_ref[...]
        y = cw[FFN_CONV - 1:FFN_CONV, :] * acc + b_ref[...]
        for j in range(FFN_CONV - 1):
            shift = FFN_CONV - 1 - j
            y = y + cw[j:j + 1, :] * e_ref[SUBLANES - shift:SUBLANES - shift + tm, :]
        e_ref[0:SUBLANES, :] = acc[tm - SUBLANES:tm, :]
        return y

    gate = conv(wg16_ref, cg_ref, bg_ref, eg_ref)
    up = conv(wu16_ref, cu_ref, bu_ref, eu_ref)
    o_ref[...] = (_silu(gate) * up).astype(o_ref.dtype)


def ffn_up(a, w_up, layer, conv_w, conv_b, seq, tm=1024, tn=512):
    m, k = a.shape
    nb = D_FF // tn
    return pl.pallas_call(
        functools.partial(_ffn_up_kernel, tiles_per_seq=seq // tm),
        grid=(nb, m // tm),
        in_specs=[pl.BlockSpec((tm, k), lambda j, i: (i, 0)),
                  _weight_spec(w_up, layer, (k, tn), lambda j, i: (0, j)),
                  _weight_spec(w_up, layer, (k, tn), lambda j, i: (0, j + nb)),
                  pl.BlockSpec((FFN_CONV, tn), lambda j, i: (0, j)),
                  pl.BlockSpec((FFN_CONV, tn), lambda j, i: (0, j + nb)),
                  pl.BlockSpec((1, tn), lambda j, i: (0, j)),
                  pl.BlockSpec((1, tn), lambda j, i: (0, j + nb))],
        out_specs=pl.BlockSpec((tm, tn), lambda j, i: (i, j)),
        out_shape=jax.ShapeDtypeStruct((m, D_FF), BF16),
        scratch_shapes=[pltpu.VMEM((SUBLANES + tm, tn), F32), pltpu.VMEM((SUBLANES + tm, tn), F32),
                        pltpu.VMEM((k, tn), BF16), pltpu.VMEM((k, tn), BF16)],
        compiler_params=_params("parallel", "arbitrary"),
        name="ffn_up",
    )(a, w_up, w_up, conv_w, conv_w, conv_b.reshape(1, -1), conv_b.reshape(1, -1))


def _prep_in_small(w_in_t, layer):
    d = w_in_t.shape[2]
    base = 4 * DN_WIDTH
    rows = lambda a, n: lax.slice(w_in_t, (layer, a, 0), (layer + 1, a + n, d)).reshape(n, d)
    wb = rows(base, DN_HEADS)
    wa = rows(base + DN_HEADS, DN_HEADS)
    mq0 = base + 2 * DN_HEADS
    wq = rows(mq0, MLA_Q_RANK)
    wckv = rows(mq0 + MLA_Q_RANK, MLA_KV_RANK)
    wkpe = rows(mq0 + MLA_Q_RANK + MLA_KV_RANK, MLA_ROPE)
    z = lambda n: jnp.zeros((n, d), w_in_t.dtype)
    return jnp.concatenate([wq, wckv, wkpe, z(LANES - MLA_ROPE), wb, wa, z(LANES - 2 * DN_HEADS)], axis=0)


def _prep_wqb(w):
    r = w.shape[0]
    w3 = w.reshape(r, MLA_HEADS, MLA_NOPE + MLA_ROPE)
    w3 = jnp.concatenate([w3, jnp.zeros((r, MLA_HEADS, LANES - MLA_ROPE), w.dtype)], axis=2)
    return w3.reshape(r, MLA_HEADS * MLA_QK)


def kernel(x, mem, positions, norm_mix, w_in, dn_conv, dn_a_log, dn_dt_bias, dn_out_norm, mla_q_norm,
           mla_w_qb, mla_kv_norm, mla_w_kvb, w_out, mem_norm, norm_xattn, xa_wq, xa_wk, xa_wv, xa_wo,
           norm_ffn, ffn_w_up, ffn_conv, ffn_conv_bias, ffn_w_down, norm_final):
    batch, seq, d = x.shape
    n_mem = mem.shape[1]
    t = batch * seq
    depth = w_in.shape[0]

    rope_c, rope_sa, rope_sb = rope_tables(positions)
    mem_n = rmsnorm_rows(mem.reshape(batch * n_mem, d), mem_norm, BF16, tm=256)
    h = x.reshape(t, d)
    u = rmsnorm_rows(h, norm_mix[0], BF16)

    w_in_t = jnp.swapaxes(w_in, 1, 2)

    for l in range(depth):
        proj_main = matmul(u, w_in_t, F32, tm=1024, tn=1024, name="in_proj_main", layer=l, n_cols=4 * DN_WIDTH,
                           w_transposed=True)
        proj_small = matmul(u, _prep_in_small(w_in_t, l), F32, tm=1024, tn=SM_WIDTH, name="in_proj_small",
                            w_transposed=True)

        o_dn = deltanet(proj_main, proj_small, dn_conv[l], dn_a_log[l], dn_dt_bias[l], dn_out_norm[l], batch, seq)

        qt, k, vt = mla_prep(proj_small, rope_c, rope_sa, rope_sb, mla_q_norm[l], mla_kv_norm[l],
                             _prep_wqb(mla_w_qb[l]).astype(BF16), mla_w_kvb[l].astype(BF16),
                             batch, seq, tm=MLA_TILE)
        o_mla = mla_attention(qt, k, vt, batch, seq, tq=MLA_TILE)

        h, u = matmul_residual_norm([o_dn, o_mla], w_out, h, norm_xattn[l], BF16, tm=256, name="out_proj", layer=l)

        xq, w_down16 = matmul(u, xa_wq, BF16, tm=1024, tn=1024, name="xa_q", layer=l, side_cast=(ffn_w_down, l))
        xk = matmul(mem_n, xa_wk, BF16, tm=1024, tn=1024, name="xa_k", layer=l)
        xv = matmul(mem_n, xa_wv, BF16, tm=1024, tn=1024, name="xa_v", layer=l)
        o_xa = cross_attention(xq, xk, xv, batch, seq, n_mem)
        h, u = matmul_residual_norm([o_xa], xa_wo, h, norm_ffn[l], BF16, tm=256, name="xa_o", layer=l)

        hid = ffn_up(u, ffn_w_up, l, ffn_conv[l], ffn_conv_bias[l], seq)
        if l + 1 < depth:
            h, u = matmul_residual_norm([hid], w_down16, h, norm_mix[l + 1], BF16, tm=256, name="ffn_down")
        else:
            out = matmul_residual_norm([hid], w_down16, h, norm_final, x.dtype, tm=256, emit_h=False,
                                       name="ffn_down_final")
    return out.reshape(batch, seq, d)
```

```python
import functools

import numpy as np
import jax
import jax.numpy as jnp
from jax import lax
from jax.experimental import pallas as pl
from jax.experimental.pallas import tpu as pltpu

F32 = jnp.float32
BF16 = jnp.bfloat16

D_MODEL = 2048
DEPTH = 4
CHUNK = 64
DN_HEADS = 8
DN_HEAD_DIM = 128
DN_WIDTH = DN_HEADS * DN_HEAD_DIM
DN_CONV = 4
MLA_HEADS = 8
MLA_NOPE = 128
MLA_ROPE = 64
MLA_V = 128
MLA_Q_RANK = 512
MLA_KV_RANK = 256
ROPE_BASE = 10000.0
XA_HEADS = 4
XA_HEAD_DIM = D_MODEL // XA_HEADS
D_FF = 5632
FFN_CONV = 3
EPS = 1e-6
LOG2_E = float(np.log2(np.e))

LANES = 128
SUBLANES = 8
BF16_SUBLANES = 16
MXU_COLS = 256
VMEM_LIMIT = 56 * 1024 * 1024

SM_Q = 0
SM_CKV = MLA_Q_RANK
SM_KPE = SM_CKV + MLA_KV_RANK
SM_BA = SM_KPE + LANES
SM_WIDTH = SM_BA + LANES
MLA_QK = MLA_NOPE + LANES
MLA_VT_ROWS = MLA_V + BF16_SUBLANES
MLA_TILE = 512
DN_CHUNKS_PER_STEP = 4


def _params(*sem):
    return pltpu.CompilerParams(dimension_semantics=sem, vmem_limit_bytes=VMEM_LIMIT)


def _dot(a, b):
    return jnp.dot(a, b, preferred_element_type=F32)


def _dot_nt(a, b):
    return lax.dot_general(a, b, (((1,), (1,)), ((), ())), preferred_element_type=F32)


def _bdot(a, b):
    return jnp.einsum("hmk,hkn->hmn", a, b, preferred_element_type=F32)


def _bdot_nt(a, b):
    return jnp.einsum("hmk,hnk->hmn", a, b, preferred_element_type=F32)


def _rms(x, gain):
    return x * lax.rsqrt(jnp.mean(x * x, axis=-1, keepdims=True) + EPS) * gain


def _silu(x):
    return x * jax.nn.sigmoid(x)


def _rmsnorm_kernel(x_ref, g_ref, o_ref):
    o_ref[...] = _rms(x_ref[...].astype(F32), g_ref[...]).astype(o_ref.dtype)


def rmsnorm_rows(x, gain, out_dtype, tm=512):
    m, d = x.shape
    return pl.pallas_call(
        _rmsnorm_kernel,
        grid=(m // tm,),
        in_specs=[pl.BlockSpec((tm, d), lambda i: (i, 0)),
                  pl.BlockSpec((1, d), lambda i: (0, 0))],
        out_specs=pl.BlockSpec((tm, d), lambda i: (i, 0)),
        out_shape=jax.ShapeDtypeStruct((m, d), out_dtype),
        compiler_params=_params("parallel"),
        name="rmsnorm_rows",
    )(x, gain.reshape(1, d))


def _weight_spec(w, layer, block, index_fn, **kw):
    if layer is None:
        return pl.BlockSpec(block, index_fn, **kw)
    return pl.BlockSpec((None,) + block, lambda *g: (layer,) + index_fn(*g), **kw)


def _mm_kernel(a_ref, w_ref, *rest, w_transposed, side_cast):
    if side_cast:
        src_ref, o_ref, dst_ref = rest[:3]
        scratch = rest[3:]
        dst_ref[...] = src_ref[...].astype(dst_ref.dtype)
    else:
        o_ref = rest[0]
        scratch = rest[1:]
    if scratch:
        w16_ref, = scratch

        @pl.when(pl.program_id(1) == 0)
        def _():
            if w_transposed:
                for c in range(0, w_ref.shape[0], MXU_COLS):
                    w16_ref[:, c:c + MXU_COLS] = w_ref[c:c + MXU_COLS, :].T.astype(BF16)
            else:
                w16_ref[...] = w_ref[...].astype(BF16)

        w = w16_ref[...]
    else:
        w = w_ref[...]
    o_ref[...] = _dot(a_ref[...], w).astype(o_ref.dtype)


def matmul(a, w, out_dtype, tm, tn, name, layer=None, n_cols=None, w_transposed=False, side_cast=None):
    m, k = a.shape
    cast = w.dtype != BF16
    assert cast or not w_transposed
    n = w.shape[-2 if w_transposed else -1] if n_cols is None else n_cols
    nm = m // tm
    if w_transposed:
        w_spec = _weight_spec(w, layer, (tn, k), lambda j, i: (j, 0))
    else:
        w_spec = _weight_spec(w, layer, (k, tn), lambda j, i: (0, j))
    in_specs = [pl.BlockSpec((tm, k), lambda j, i: (i, 0)), w_spec]
    out_specs = [pl.BlockSpec((tm, tn), lambda j, i: (i, j))]
    out_shape = [jax.ShapeDtypeStruct((m, n), out_dtype)]
    operands = [a, w]
    if side_cast is not None:
        src, src_layer = side_cast
        rows, cols = src.shape[1] // ((n // tn) * nm), src.shape[2]
        in_specs.append(pl.BlockSpec((None, rows, cols), lambda j, i: (src_layer, j * nm + i, 0)))
        out_specs.append(pl.BlockSpec((rows, cols), lambda j, i: (j * nm + i, 0)))
        out_shape.append(jax.ShapeDtypeStruct(src.shape[1:], BF16))
        operands.append(src)
    res = pl.pallas_call(
        functools.partial(_mm_kernel, w_transposed=w_transposed, side_cast=side_cast is not None),
        grid=(n // tn, nm),
        in_specs=in_specs,
        out_specs=out_specs,
        out_shape=out_shape,
        scratch_shapes=[pltpu.VMEM((k, tn), BF16)] if cast else [],
        compiler_params=_params("parallel", "arbitrary"),
        name=name,
    )(*operands)
    return res[0] if side_cast is None else tuple(res)


def _mm_layers_kernel(a_ref, w_ref, o_ref):
    o_ref[...] = _dot(a_ref[...], w_ref[...].astype(BF16)).astype(o_ref.dtype)


def matmul_all_layers(a, w, out_dtype, tn, name):
    m, k = a.shape
    depth, _, n = w.shape
    return pl.pallas_call(
        _mm_layers_kernel,
        grid=(depth, n // tn),
        in_specs=[pl.BlockSpec((m, k), lambda l, j: (0, 0)),
                  pl.BlockSpec((None, k, tn), lambda l, j: (l, 0, j))],
        out_specs=pl.BlockSpec((None, m, tn), lambda l, j: (l, 0, j)),
        out_shape=jax.ShapeDtypeStruct((depth, m, n), out_dtype),
        compiler_params=_params("parallel", "parallel"),
        name=name,
    )(a, w)


def _mm_res_norm_kernel(*refs, n_parts, emit_h, cast):
    a_refs = refs[:n_parts]
    w_ref, h_ref, g_ref = refs[n_parts:n_parts + 3]
    outs = refs[n_parts + 3:]
    ho_ref = outs[0] if emit_h else None
    uo_ref = outs[1] if emit_h else outs[0]
    if cast:
        w16_ref = outs[-1]

        @pl.when(pl.program_id(0) == 0)
        def _():
            w16_ref[...] = w_ref[...].astype(BF16)

        w_ref = w16_ref
    off = 0
    acc = None
    for a_ref in a_refs:
        kp = a_ref.shape[1]
        part = _dot(a_ref[...], w_ref[off:off + kp, :])
        acc = part if acc is None else acc + part
        off += kp
    h = h_ref[...] + acc
    if emit_h:
        ho_ref[...] = h
    uo_ref[...] = _rms(h, g_ref[...]).astype(uo_ref.dtype)


def matmul_residual_norm(a_parts, w, h, gain, u_dtype, tm, name, layer=None, emit_h=True):
    m, n = h.shape
    k = w.shape[-2]
    n_parts = len(a_parts)
    cast = w.dtype != BF16
    row = lambda i: (i, 0)
    in_specs = [pl.BlockSpec((tm, a.shape[1]), row) for a in a_parts] + [
        _weight_spec(w, layer, (k, n), lambda i: (0, 0), pipeline_mode=pl.Buffered(1)),
        pl.BlockSpec((tm, n), row),
        pl.BlockSpec((1, n), lambda i: (0, 0)),
    ]
    out_specs = [pl.BlockSpec((tm, n), row)]
    out_shape = [jax.ShapeDtypeStruct((m, n), u_dtype)]
    if emit_h:
        out_specs = [pl.BlockSpec((tm, n), row)] + out_specs
        out_shape = [jax.ShapeDtypeStruct((m, n), F32)] + out_shape
    res = pl.pallas_call(
        functools.partial(_mm_res_norm_kernel, n_parts=n_parts, emit_h=emit_h, cast=cast),
        grid=(m // tm,),
        in_specs=in_specs,
        out_specs=out_specs,
        out_shape=out_shape,
        scratch_shapes=[pltpu.VMEM((k, n), BF16)] if cast else [],
        compiler_params=_params("arbitrary"),
        name=name,
    )(*a_parts, w, h, gain.reshape(1, n))
    return tuple(res) if emit_h else res[0]


def _rope_table_kernel(pos_ref, inv_ref, c_ref, sa_ref, sb_ref):
    ang = pos_ref[...].astype(F32) * inv_ref[...]
    c = jnp.cos(ang)
    s = jnp.sin(ang)
    lane = lax.broadcasted_iota(jnp.int32, ang.shape, 1)
    half = MLA_ROPE // 2
    c_ref[...] = jnp.where(lane < MLA_ROPE, c, 0.0)
    sa_ref[...] = jnp.where(lane < half, -s, 0.0)
    sb_ref[...] = jnp.where((lane >= half) & (lane < MLA_ROPE), s, 0.0)


def rope_tables(positions, tm=512):
    t = positions.size
    inv = ROPE_BASE ** (-jnp.arange(0, MLA_ROPE, 2, dtype=F32) / MLA_ROPE)
    inv_row = jnp.concatenate([inv, inv, jnp.zeros((LANES - MLA_ROPE,), F32)]).reshape(1, LANES)
    spec = pl.BlockSpec((tm, LANES), lambda i: (i, 0))
    shape = jax.ShapeDtypeStruct((t, LANES), F32)
    return pl.pallas_call(
        _rope_table_kernel,
        grid=(t // tm,),
        in_specs=[pl.BlockSpec((tm, 1), lambda i: (i, 0)),
                  pl.BlockSpec((1, LANES), lambda i: (0, 0))],
        out_specs=[spec, spec, spec],
        out_shape=[shape, shape, shape],
        compiler_params=_params("parallel"),
        name="rope_tables",
    )(positions.reshape(t, 1), inv_row)


def _rope(x, c, sa, sb):
    half = MLA_ROPE // 2
    return x * c + pltpu.roll(x, LANES - half, 1) * sa + pltpu.roll(x, half, 1) * sb


def _deltanet_kernel(x_ref, ba_ref, cw_ref, alog_ref, dtb_ref, onorm_ref, o_ref, xs_ref, state_ref, *, nch):
    C, D, H = CHUNK, DN_HEAD_DIM, DN_HEADS
    W3 = 3 * DN_WIDTH
    TS = nch * C
    pairs = [(c, h) for c in range(nch) for h in range(H)]

    @pl.when(pl.program_id(1) == 0)
    def _():
        xs_ref[0:SUBLANES, :] = jnp.zeros((SUBLANES, W3), F32)
        state_ref[...] = jnp.zeros_like(state_ref)

    x = x_ref[:, 0:W3]
    xs_ref[SUBLANES:SUBLANES + TS, :] = x
    cw = cw_ref[...]
    y = cw[DN_CONV - 1:DN_CONV, :] * x
    for j in range(DN_CONV - 1):
        shift = DN_CONV - 1 - j
        y = y + cw[j:j + 1, :] * xs_ref[SUBLANES - shift:SUBLANES - shift + TS, :]
    xs_ref[0:SUBLANES, :] = x[TS - SUBLANES:TS, :]
    y = _silu(y)

    ba = ba_ref[...]
    beta_all = jax.nn.sigmoid(ba)
    sp_in = ba + dtb_ref[...]
    softplus = jnp.maximum(sp_in, 0.0) + jnp.log1p(jnp.exp(-jnp.abs(sp_in)))
    g_all = -jnp.exp(alog_ref[...]) * softplus
    row_in_chunk = lax.broadcasted_iota(jnp.int32, (TS, LANES), 0) & (C - 1)
    G = g_all
    step = 1
    while step < C:
        G = G + jnp.where(row_in_chunk >= step, pltpu.roll(G, step, 0), 0.0)
        step *= 2
    Gc = [G[c * C:(c + 1) * C, :] for c in range(nch)]
    GTc = [g.T for g in Gc]
    g_last = [g[C - 1:C, :] for g in Gc]
    e_g = jnp.exp(G)
    e_rem = [jnp.exp(g_last[c] - Gc[c]) for c in range(nch)]
    e_last = [jnp.exp(g) for g in g_last]

    ri = lax.broadcasted_iota(jnp.int32, (C, C), 0)
    ci = lax.broadcasted_iota(jnp.int32, (C, C), 1)
    incl = (ri >= ci)[None]
    strict = (ri > ci)[None]
    eye = jnp.where(ri == ci, 1.0, 0.0).astype(F32)[None]
    onorm = onorm_ref[...]

    def heads(off):
        return jnp.stack([y[c * C:(c + 1) * C, off + h * D:off + (h + 1) * D] for c, h in pairs])

    def head_cols(a, lane0):
        return jnp.stack([a[c * C:(c + 1) * C, lane0 + h:lane0 + h + 1] for c, h in pairs])

    q = heads(0)
    k = heads(DN_WIDTH)
    v = heads(2 * DN_WIDTH)
    q = q * lax.rsqrt(jnp.sum(q * q, axis=-1, keepdims=True) + EPS) * (D ** -0.5)
    k = k * lax.rsqrt(jnp.sum(k * k, axis=-1, keepdims=True) + EPS)
    beta = head_cols(beta_all, 0)
    g_col = head_cols(G, H)
    g_row = jnp.stack([GTc[c][H + h:H + h + 1, :] for c, h in pairs])
    eg_col = head_cols(e_g, H)
    erem_col = jnp.stack([e_rem[c][:, H + h:H + h + 1] for c, h in pairs])
    elast = jnp.stack([e_last[c][:, H + h:H + h + 1] for c, h in pairs])
    decay = jnp.exp(jnp.where(incl, g_col - g_row, -jnp.inf))
    kb = k * beta
    k16 = k.astype(BF16)
    kq = _bdot_nt(jnp.concatenate([kb.astype(BF16), q.astype(BF16)], axis=1), k16)
    lower = jnp.where(strict, kq[:, 0:C] * decay, 0.0)
    attn16 = (kq[:, C:2 * C] * decay).astype(BF16)

    def same_block(bits):
        return (ri >> bits) == (ci >> bits)

    base_bits = SUBLANES.bit_length() - 1
    ld =jnp.where((strict[0] & same_block(base_bits))[None], lower, 0.0)
    ld16 = ld.astype(BF16)
    t_inv = eye - ld
    p16 = _bdot(ld16, ld16).astype(BF16)
    r = _bdot(jnp.concatenate([t_inv.astype(BF16), p16], axis=1), p16)
    t_inv = t_inv + r[:, 0:C]
    t_inv = t_inv + _bdot(t_inv.astype(BF16), r[:, C:2 * C].astype(BF16))
    for bits in range(base_bits, C.bit_length() - 1):
        off_diag = (strict[0] & same_block(bits + 1) & jnp.logical_not(same_block(bits)))[None]
        t16 = t_inv.astype(BF16)
        x = _bdot(t16, jnp.where(off_diag, lower, 0.0).astype(BF16))
        t_inv = t_inv - _bdot(x.astype(BF16), t16)

    rhs = jnp.concatenate([v * beta, kb * eg_col], axis=2)
    sol = _bdot(t_inv.astype(BF16), rhs.astype(BF16))
    u = sol[:, :, 0:D]
    wq16 = jnp.concatenate([sol[:, :, D:2 * D].astype(BF16), (q * eg_col).astype(BF16)], axis=1)
    kdec16 = (k * erem_col).astype(BF16)

    s = state_ref[...]
    for c in range(nch):
        sl = slice(c * H, (c + 1) * H)
        ws = _bdot(wq16[sl], s.astype(BF16))
        v_new = u[sl] - ws[:, 0:C]
        vn16 = v_new.astype(BF16)
        o = ws[:, C:2 * C] + _bdot(attn16[sl], vn16)
        s = s * elast[sl] + jnp.einsum("hck,hcv->hkv", kdec16[sl], vn16, preferred_element_type=F32)
        on = _rms(o, onorm[None])
        for h in range(H):
            z = x_ref[c * C:(c + 1) * C, W3 + h * D:W3 + (h + 1) * D]
            o_ref[c * C:(c + 1) * C, h * D:(h + 1) * D] = (on[h] * _silu(z)).astype(o_ref.dtype)
    state_ref[...] = s


def deltanet(proj_main, proj_small, conv_w, a_log, dt_bias, out_norm, batch, seq, nch=DN_CHUNKS_PER_STEP):
    t = batch * seq
    ts = nch * CHUNK
    ns = seq // ts
    pad = jnp.zeros((LANES - 2 * DN_HEADS,), F32)
    alog_row = jnp.concatenate([jnp.zeros((DN_HEADS,), F32), a_log, pad]).reshape(1, LANES)
    dtb_row = jnp.concatenate([jnp.zeros((DN_HEADS,), F32), dt_bias, pad]).reshape(1, LANES)
    w4 = 4 * DN_WIDTH
    return pl.pallas_call(
        functools.partial(_deltanet_kernel, nch=nch),
        grid=(batch, ns),
        in_specs=[pl.BlockSpec((ts, w4), lambda b, c: (b * ns + c, 0)),
                  pl.BlockSpec((ts, LANES), lambda b, c: (b * ns + c, SM_BA // LANES)),
                  pl.BlockSpec((DN_CONV, 3 * DN_WIDTH), lambda b, c: (0, 0)),
                  pl.BlockSpec((1, LANES), lambda b, c: (0, 0)),
                  pl.BlockSpec((1, LANES), lambda b, c: (0, 0)),
                  pl.BlockSpec((1, DN_HEAD_DIM), lambda b, c: (0, 0))],
        out_specs=pl.BlockSpec((ts, DN_WIDTH), lambda b, c: (b * ns + c, 0)),
        out_shape=jax.ShapeDtypeStruct((t, DN_WIDTH), BF16),
        scratch_shapes=[pltpu.VMEM((SUBLANES + ts, 3 * DN_WIDTH), F32),
                        pltpu.VMEM((DN_HEADS, DN_HEAD_DIM, DN_HEAD_DIM), F32)],
        compiler_params=_params("arbitrary", "arbitrary"),
        name="deltanet",
    )(proj_main, proj_small, conv_w, alog_row, dtb_row, out_norm.reshape(1, DN_HEAD_DIM))


def _mla_prep_kernel(x_ref, c_ref, sa_ref, sb_ref, qn_ref, kvn_ref, wq_ref, wkv_ref, qt_ref, k_ref, vt_ref):
    c, sa, sb = c_ref[...], sa_ref[...], sb_ref[...]
    scale = (MLA_NOPE + MLA_ROPE) ** -0.5 * LOG2_E
    q_lat = _rms(x_ref[:, SM_Q:SM_Q + MLA_Q_RANK], qn_ref[...])
    qf = _dot(q_lat.astype(BF16), wq_ref[...]) * scale
    c_kv = _rms(x_ref[:, SM_CKV:SM_CKV + MLA_KV_RANK], kvn_ref[...])
    kvf = _dot(c_kv.astype(BF16), wkv_ref[...])
    k_pe = _rope(x_ref[:, SM_KPE:SM_KPE + LANES], c, sa, sb).astype(k_ref.dtype)
    pad_rows = MLA_VT_ROWS - MLA_V
    ones_row = (lax.broadcasted_iota(jnp.int32, (pad_rows, x_ref.shape[0]), 0) == 0).astype(vt_ref.dtype)
    for h in range(MLA_HEADS):
        o = h * MLA_QK
        q_h = jnp.concatenate([qf[:, o:o + MLA_NOPE], _rope(qf[:, o + MLA_NOPE:o + MLA_QK], c, sa, sb)], axis=1)
        qt_ref[o:o + MLA_QK, :] = q_h.T.astype(qt_ref.dtype)
        k_ref[:, o:o + MLA_NOPE] = kvf[:, o:o + MLA_NOPE].astype(k_ref.dtype)
        k_ref[:, o + MLA_NOPE:o + MLA_QK] = k_pe
        vt_ref[h * MLA_VT_ROWS:h * MLA_VT_ROWS + MLA_V, :] = kvf[:, o + MLA_NOPE:o + MLA_QK].T.astype(vt_ref.dtype)
        vt_ref[h * MLA_VT_ROWS + MLA_V:(h + 1) * MLA_VT_ROWS, :] = ones_row


def mla_prep(proj_small, rope_c, rope_sa, rope_sb, q_norm, kv_norm, wq, wkv, batch, seq, tm):
    t = proj_small.shape[0]
    ns = seq // tm
    hq = MLA_HEADS * MLA_QK
    hv = MLA_HEADS * MLA_VT_ROWS
    row = lambda i: (i, 0)
    fixed = lambda i: (0, 0)
    return pl.pallas_call(
        _mla_prep_kernel,
        grid=(t // tm,),
        in_specs=[pl.BlockSpec((tm, SM_WIDTH), row),
                  pl.BlockSpec((tm, LANES), row), pl.BlockSpec((tm, LANES), row), pl.BlockSpec((tm, LANES), row),
                  pl.BlockSpec((1, MLA_Q_RANK), fixed), pl.BlockSpec((1, MLA_KV_RANK), fixed),
                  pl.BlockSpec((MLA_Q_RANK, hq), fixed), pl.BlockSpec((MLA_KV_RANK, hq), fixed)],
        out_specs=[pl.BlockSpec((None, hq, tm), lambda i: (i // ns, 0, i % ns)),
                   pl.BlockSpec((tm, hq), row),
                   pl.BlockSpec((None, None, hv, tm), lambda i: (i // ns, i % ns, 0, 0))],
        out_shape=[jax.ShapeDtypeStruct((batch, hq, seq), BF16),
                   jax.ShapeDtypeStruct((t, hq), BF16),
                   jax.ShapeDtypeStruct((batch, ns, hv, tm), BF16)],
        compiler_params=_params("parallel"),
        name="mla_prep",
    )(proj_small, rope_c, rope_sa, rope_sb, q_norm.reshape(1, -1), kv_norm.reshape(1, -1), wq, wkv)


def _mla_attn_kernel(qt_ref, k_ref, vt_ref, o_ref, *, tq):
    qi = pl.program_id(1)
    H = MLA_HEADS
    qt = qt_ref[...].reshape(H, MLA_QK, tq)

    def scores(j):
        start = pl.multiple_of(j * tq, tq)
        kk = k_ref[pl.ds(start, tq), :]
        kh = jnp.stack([kk[:, h * MLA_QK:(h + 1) * MLA_QK] for h in range(H)])
        return _bdot(kh, qt)

    def update(carry, s, vtj):
        m, acc = carry
        m_new = jnp.maximum(m, jnp.max(s, axis=1, keepdims=True))
        alpha = jnp.exp2(m - m_new)
        p = jnp.exp2(s - m_new)
        acc = alpha * acc + _bdot(vtj.reshape(H, MLA_VT_ROWS, tq), p.astype(BF16))
        return m_new, acc

    def body(j, carry):
        return update(carry, scores(j), vt_ref[j])

    init = (jnp.full((H, 1, tq), -jnp.inf, F32), jnp.zeros((H, MLA_VT_ROWS, tq), F32))
    carry = lax.fori_loop(0, qi, body, init)
    chunk_bits = CHUNK.bit_length() - 1
    kc = lax.shift_right_logical(lax.broadcasted_iota(jnp.int32, (tq, tq), 0), chunk_bits)
    qc = lax.shift_right_logical(lax.broadcasted_iota(jnp.int32, (tq, tq), 1), chunk_bits)
    s = jnp.where((kc <= qc)[None], scores(qi), -jnp.inf)
    _, acc = update(carry, s, vt_ref[qi])
    o = acc[:, 0:MLA_V] / acc[:, MLA_V:MLA_V + 1]
    for h in range(H):
        o_ref[:, h * MLA_V:(h + 1) * MLA_V] = o[h].T.astype(o_ref.dtype)


def mla_attention(qt, k, vt, batch, seq, tq):
    t = batch * seq
    nq = seq // tq
    hq = MLA_HEADS * MLA_QK
    hv = MLA_HEADS * MLA_V
    hvt = MLA_HEADS * MLA_VT_ROWS
    return pl.pallas_call(
        functools.partial(_mla_attn_kernel, tq=tq),
        grid=(batch, nq),
        in_specs=[pl.BlockSpec((None, hq, tq), lambda b, i: (b, 0, i)),
                  pl.BlockSpec((seq, hq), lambda b, i: (b, 0)),
                  pl.BlockSpec((None, nq, hvt, tq), lambda b, i: (b, 0, 0, 0))],
        out_specs=pl.BlockSpec((tq, hv), lambda b, i: (b * nq + i, 0)),
        out_shape=jax.ShapeDtypeStruct((t, hv), BF16),
        compiler_params=_params("parallel", "parallel"),
        name="mla_attention",
    )(qt, k, vt)


def _xattn_kernel(q_ref, k_ref, v_ref, o_ref):
    scale = XA_HEAD_DIM ** -0.5
    for h in range(XA_HEADS):
        sl = slice(h * XA_HEAD_DIM, (h + 1) * XA_HEAD_DIM)
        s = _dot_nt(q_ref[:, sl], k_ref[:, sl]) * scale
        p = jnp.exp(s - jnp.max(s, axis=-1, keepdims=True))
        l = jnp.sum(p, axis=-1, keepdims=True)
        o_ref[:, sl] = (_dot(p.astype(BF16), v_ref[:, sl]) / l).astype(o_ref.dtype)


def cross_attention(q, k, v, layer, batch, seq, n_mem, tq=512):
    t, d = q.shape
    nq = seq // tq
    return pl.pallas_call(
        _xattn_kernel,
        grid=(batch, nq),
        in_specs=[pl.BlockSpec((tq, d), lambda b, i: (b * nq + i, 0)),
                  pl.BlockSpec((None, n_mem, d), lambda b, i: (layer, b, 0)),
                  pl.BlockSpec((None, n_mem, d), lambda b, i: (layer, b, 0))],
        out_specs=pl.BlockSpec((tq, d), lambda b, i: (b * nq + i, 0)),
        out_shape=jax.ShapeDtypeStruct((t, d), BF16),
        compiler_params=_params("parallel", "parallel"),
        name="cross_attention",
    )(q, k, v)


def _ffn_up_kernel(a_ref, wg_ref, wu_ref, cg_ref, cu_ref, bg_ref, bu_ref, o_ref, eg_ref, eu_ref, wg16_ref, wu16_ref,
                   *, tiles_per_seq):
    i = pl.program_id(1)
    tm = a_ref.shape[0]

    @pl.when(i == 0)
    def _():
        wg16_ref[...] = wg_ref[...].astype(BF16)
        wu16_ref[...] = wu_ref[...].astype(BF16)

    @pl.when(i % tiles_per_seq == 0)
    def _():
        eg_ref[0:SUBLANES, :] = jnp.zeros((SUBLANES, eg_ref.shape[1]), F32)
        eu_ref[0:SUBLANES, :] = jnp.zeros((SUBLANES, eu_ref.shape[1]), F32)

    a = a_ref[...]

    def conv(w_ref, cw_ref, b_ref, e_ref):
        acc = _dot(a, w_ref[...])
        e_ref[SUBLANES:SUBLANES + tm, :] = acc
        cw = cw_ref[...]
        y = cw[FFN_CONV - 1:FFN_CONV, :] * acc + b_ref[...]
        for j in range(FFN_CONV - 1):
            shift = FFN_CONV - 1 - j
            y = y + cw[j:j + 1, :] * e_ref[SUBLANES - shift:SUBLANES - shift + tm, :]
        e_ref[0:SUBLANES, :] = acc[tm - SUBLANES:tm, :]
        return y

    gate = conv(wg16_ref, cg_ref, bg_ref, eg_ref)
    up = conv(wu16_ref, cu_ref, bu_ref, eu_ref)
    o_ref[...] = (_silu(gate) * up).astype(o_ref.dtype)


def ffn_up(a, w_up, layer, conv_w, conv_b, seq, tm=1024, tn=512):
    m, k = a.shape
    nb = D_FF // tn
    return pl.pallas_call(
        functools.partial(_ffn_up_kernel, tiles_per_seq=seq // tm),
        grid=(nb, m // tm),
        in_specs=[pl.BlockSpec((tm, k), lambda j, i: (i, 0)),
                  _weight_spec(w_up, layer, (k, tn), lambda j, i: (0, j)),
                  _weight_spec(w_up, layer, (k, tn), lambda j, i: (0, j + nb)),
                  pl.BlockSpec((FFN_CONV, tn), lambda j, i: (0, j)),
                  pl.BlockSpec((FFN_CONV, tn), lambda j, i: (0, j + nb)),
                  pl.BlockSpec((1, tn), lambda j, i: (0, j)),
                  pl.BlockSpec((1, tn), lambda j, i: (0, j + nb))],
        out_specs=pl.BlockSpec((tm, tn), lambda j, i: (i, j)),
        out_shape=jax.ShapeDtypeStruct((m, D_FF), BF16),
        scratch_shapes=[pltpu.VMEM((SUBLANES + tm, tn), F32), pltpu.VMEM((SUBLANES + tm, tn), F32),
                        pltpu.VMEM((k, tn), BF16), pltpu.VMEM((k, tn), BF16)],
        compiler_params=_params("parallel", "arbitrary"),
        name="ffn_up",
    )(a, w_up, w_up, conv_w, conv_w, conv_b.reshape(1, -1), conv_b.reshape(1, -1))


def _prep_in_small(w_in_t, layer):
    d = w_in_t.shape[2]
    base = 4 * DN_WIDTH
    rows = lambda a, n: lax.slice(w_in_t, (layer, a, 0), (layer + 1, a + n, d)).reshape(n, d)
    wb = rows(base, DN_HEADS)
    wa = rows(base + DN_HEADS, DN_HEADS)
    mq0 = base + 2 * DN_HEADS
    wq = rows(mq0, MLA_Q_RANK)
    wckv = rows(mq0 + MLA_Q_RANK, MLA_KV_RANK)
    wkpe = rows(mq0 + MLA_Q_RANK + MLA_KV_RANK, MLA_ROPE)
    z = lambda n: jnp.zeros((n, d), w_in_t.dtype)
    return jnp.concatenate([wq, wckv, wkpe, z(LANES - MLA_ROPE), wb, wa, z(LANES - 2 * DN_HEADS)], axis=0)


def _prep_wqb(w):
    r = w.shape[0]
    w3 = w.reshape(r, MLA_HEADS, MLA_NOPE + MLA_ROPE)
    w3 = jnp.concatenate([w3, jnp.zeros((r, MLA_HEADS, LANES - MLA_ROPE), w.dtype)], axis=2)
    return w3.reshape(r, MLA_HEADS * MLA_QK)


def kernel(x, mem, positions, norm_mix, w_in, dn_conv, dn_a_log, dn_dt_bias, dn_out_norm, mla_q_norm,
           mla_w_qb, mla_kv_norm, mla_w_kvb, w_out, mem_norm, norm_xattn, xa_wq, xa_wk, xa_wv, xa_wo,
           norm_ffn, ffn_w_up, ffn_conv, ffn_conv_bias, ffn_w_down, norm_final):
    batch, seq, d = x.shape
    n_mem = mem.shape[1]
    t = batch * seq
    depth = w_in.shape[0]

    rope_c, rope_sa, rope_sb = rope_tables(positions)
    mem_n = rmsnorm_rows(mem.reshape(batch * n_mem, d), mem_norm, BF16, tm=256)
    h = x.reshape(t, d)
    u = rmsnorm_rows(h, norm_mix[0], BF16)

    w_in_t = jnp.swapaxes(w_in, 1, 2)
    xk_all = matmul_all_layers(mem_n, xa_wk, BF16, tn=1024, name="xa_k")
    xv_all = matmul_all_layers(mem_n, xa_wv, BF16, tn=1024, name="xa_v")

    for l in range(depth):
        proj_main = matmul(u, w_in_t, F32, tm=1024, tn=1024, name="in_proj_main", layer=l, n_cols=4 * DN_WIDTH,
                           w_transposed=True)
        proj_small = matmul(u, _prep_in_small(w_in_t, l), F32, tm=1024, tn=SM_WIDTH, name="in_proj_small",
                            w_transposed=True)

        o_dn = deltanet(proj_main, proj_small, dn_conv[l], dn_a_log[l], dn_dt_bias[l], dn_out_norm[l], batch, seq)

        qt, k, vt = mla_prep(proj_small, rope_c, rope_sa, rope_sb, mla_q_norm[l], mla_kv_norm[l],
                             _prep_wqb(mla_w_qb[l]).astype(BF16), mla_w_kvb[l].astype(BF16),
                             batch, seq, tm=MLA_TILE)
        o_mla = mla_attention(qt, k, vt, batch, seq, tq=MLA_TILE)

        h, u = matmul_residual_norm([o_dn, o_mla], w_out, h, norm_xattn[l], BF16, tm=256, name="out_proj", layer=l)

        xq, w_down16 = matmul(u, xa_wq, BF16, tm=1024, tn=1024, name="xa_q", layer=l, side_cast=(ffn_w_down, l))
        o_xa = cross_attention(xq, xk_all, xv_all, l, batch, seq, n_mem)
        h, u = matmul_residual_norm([o_xa], xa_wo, h, norm_ffn[l], BF16, tm=256, name="xa_o", layer=l)

        hid = ffn_up(u, ffn_w_up, l, ffn_conv[l], ffn_conv_bias[l], seq)
        if l + 1 < depth:
            h, u = matmul_residual_norm([hid], w_down16, h, norm_mix[l + 1], BF16, tm=256, name="ffn_down")
        else:
            out = matmul_residual_norm([hid], w_down16, h, norm_final, x.dtype, tm=256, emit_h=False,
                                       name="ffn_down_final")
    return out.reshape(batch, seq, d)
```
